```python
import jax, jax.numpy as jnp
from jax import lax
import numpy as np

D_MODEL = 2048
BATCH = 4
SEQ = 2048
DEPTH = 2
DEC_BATCH = 128
DEC_SEQ = 1
PAST_LEN = 16384
PAGE_SIZE = 128

F32 = jnp.float32
N_A = DEPTH // 2
N_B = DEPTH - N_A
A_HEAD = 64
A_HEADS = D_MODEL // A_HEAD
DECAY_LORA = 96
AAA_LORA = 96
GATE_LORA = 256
GN_EPS = 64e-5
B_HEADS = 16
QK_NOPE = 128
QK_ROPE = 64
V_HEAD = 128
KV_LORA = 512
Q_LORA = 512
ROPE_THETA = 10000.0
Q_BLOCK = 128
N_GROUPS = 4
EXP_PER_GROUP = 8
N_EXPERTS = N_GROUPS * EXP_PER_GROUP
TOP_K = 2
D_EXPERT = 512
PLE_DIM = 256
RMS_EPS = 1e-6

kernel_name = 'rwkv7_mla_yoco_hmoe_step'


def _rmsnorm(x, g):
    xf = x.astype(F32)
    y = xf * lax.rsqrt(jnp.mean(xf * xf, -1, keepdims=True) + RMS_EPS)
    return (y * g.astype(F32)).astype(x.dtype)


def _rope(x, pos):
    r = x.shape[-1]
    half = r // 2
    inv = ROPE_THETA ** (-2.0 * jnp.arange(half, dtype=F32) / r)
    ang = pos.astype(F32)[:, None] * inv[None, :]
    cos = jnp.cos(ang)[None, :, None, :]
    sin = jnp.sin(ang)[None, :, None, :]
    xf = x.astype(F32)
    x1, x2 = xf[..., :half], xf[..., half:]
    return jnp.concatenate([x1 * cos - x2 * sin, x2 * cos + x1 * sin], -1).astype(x.dtype)


def _wkv7_step(S, inp):
    r, w, k, v, a, b = inp
    sa = jnp.einsum('bhvk,bhk->bhv', S, a)
    S = S * w[:, :, None, :] + sa[..., None] * b[:, :, None, :] + v[..., None] * k[:, :, None, :]
    return S, jnp.einsum('bhvk,bhk->bhv', S, r)


def _rwkv7_time_mix(xn, x_prev, S0, mu, w_r, w_k, w_v, w_o, w0, w1, w2, a0, a1, a2, g1, g2,
                    k_k, k_a, r_k, lnx_w, lnx_b):
    Bn, T, D = xn.shape
    shifted = jnp.concatenate([x_prev[:, None, :].astype(xn.dtype), xn[:, :-1]], axis=1)
    xs = xn[:, :, None, :] + (shifted - xn)[:, :, None, :] * mu
    xr, xw, xk, xv, xa, xg = [xs[:, :, j] for j in range(6)]
    r = xr @ w_r
    k = xk @ w_k
    v = xv @ w_v
    logw = -jax.nn.softplus(-(w0 + jnp.tanh(xw @ w1) @ w2)) - 0.5
    decay = jnp.exp(-jnp.exp(logw.astype(F32)))
    a = jax.nn.sigmoid(a0 + (xa @ a1) @ a2)
    g = jax.nn.sigmoid(xg @ g1) @ g2
    heads = lambda t: t.astype(F32).reshape(Bn, T, A_HEADS, A_HEAD)
    kk = heads(k * k_k)
    kk = kk / jnp.maximum(jnp.linalg.norm(kk, axis=-1, keepdims=True), 1e-12)
    k = k * (1.0 + (a - 1.0) * k_a)
    r_h, k_h, v_h, a_h, w_h = heads(r), heads(k), heads(v), heads(a), heads(decay)
    tm = lambda t: jnp.swapaxes(t, 0, 1)
    S_fin, y = lax.scan(_wkv7_step, S0.astype(F32),
                        (tm(r_h), tm(w_h), tm(k_h), tm(v_h), tm(-kk), tm(kk * a_h)))
    y = tm(y)
    mean = jnp.mean(y, -1, keepdims=True)
    var = jnp.mean(jnp.square(y - mean), -1, keepdims=True)
    y = ((y - mean) * lax.rsqrt(var + GN_EPS)).reshape(Bn, T, D) * lnx_w.astype(F32) + lnx_b.astype(F32)
    bonus = jnp.sum(r_h * k_h * r_k.astype(F32), -1, keepdims=True) * v_h
    y = (y + bonus.reshape(Bn, T, D)) * g.astype(F32)
    return y.astype(xn.dtype) @ w_o, S_fin, xn[:, -1]


def _shared_kv(h, pos, g_in, w_down, g_latent):
    c = _rmsnorm(h, g_in) @ w_down
    lat = _rmsnorm(c[..., :KV_LORA], g_latent)
    kpe = _rope(c[..., None, KV_LORA:], pos)[:, :, 0]
    return lat, kpe


def _mla_queries(xn, pos, w_dq, g_q, w_uq, w_uk):
    Bn, T, _ = xn.shape
    cq = _rmsnorm(xn @ w_dq, g_q)
    q = (cq @ w_uq).reshape(Bn, T, B_HEADS, QK_NOPE + QK_ROPE)
    q_pe = _rope(q[..., QK_NOPE:], pos)
    scale = (QK_NOPE + QK_ROPE) ** -0.5
    q_abs = jnp.einsum('bthd,chd->bthc', q[..., :QK_NOPE], w_uk).astype(F32) * scale
    return q_abs, q_pe.astype(F32) * scale


def _attn_partial(q_abs, q_pe, lat, kpe, mask):
    s = jnp.einsum('bqhc,bkc->bhqk', q_abs, lat) + jnp.einsum('bqhr,bkr->bhqk', q_pe, kpe)
    if mask is not None:
        s = jnp.where(mask[:, None], s, -jnp.inf)
    m = jnp.max(s, -1)
    p = jnp.exp(s - m[..., None])
    return m, jnp.sum(p, -1), jnp.einsum('bhqk,bkc->bhqc', p, lat)


def _merge(s1, s2):
    m1, l1, a1 = s1
    m2, l2, a2 = s2
    m = jnp.maximum(m1, m2)
    c1, c2 = jnp.exp(m1 - m), jnp.exp(m2 - m)
    return m, l1 * c1 + l2 * c2, a1 * c1[..., None] + a2 * c2[..., None]


def _mla_out(o_lat, w_uv, w_o, dtype):
    o = jnp.einsum('bhqc,chd->bqhd', o_lat.astype(dtype), w_uv)
    return o.reshape(o.shape[0], o.shape[1], B_HEADS * V_HEAD) @ w_o


def _mla_prompt(xn, pos, lat, kpe, w_dq, g_q, w_uq, w_uk, w_uv, w_o):
    Bn, S, D = xn.shape
    q_abs, q_pe = _mla_queries(xn, pos, w_dq, g_q, w_uq, w_uk)
    nb = S // Q_BLOCK
    blk = lambda t: jnp.swapaxes(t.reshape((Bn, nb, Q_BLOCK) + t.shape[2:]), 0, 1)
    latf, kpef = lat.astype(F32), kpe.astype(F32)
    kpos = jnp.arange(S)

    def one_block(args):
        qa, qp, start = args
        mask = (kpos[None, :] <= (start + jnp.arange(Q_BLOCK))[:, None])[None]
        m, l, acc = _attn_partial(qa, qp, latf, kpef, mask)
        return _mla_out(acc / l[..., None], w_uv, w_o, xn.dtype)

    out = lax.map(one_block, (blk(q_abs), blk(q_pe), jnp.arange(nb) * Q_BLOCK))
    return jnp.swapaxes(out, 0, 1).reshape(Bn, S, D)


def _mla_decode(xn, pos, lat_new, kpe_new, cache_latent, cache_kpe, page_table,
                w_dq, g_q, w_uq, w_uk, w_uv, w_o):
    Bn, T, D = xn.shape
    q_abs, q_pe = _mla_queries(xn, pos, w_dq, g_q, w_uq, w_uk)
    init = (jnp.full((Bn, B_HEADS, T), -jnp.inf, F32), jnp.zeros((Bn, B_HEADS, T), F32),
            jnp.zeros((Bn, B_HEADS, T, KV_LORA), F32))

    def page_step(carry, pages):
        part = _attn_partial(q_abs, q_pe, cache_latent[pages].astype(F32),
                             cache_kpe[pages].astype(F32), None)
        return _merge(carry, part), None

    carry, _ = lax.scan(page_step, init, page_table.T)
    mask = (jnp.arange(T)[None, :] <= jnp.arange(T)[:, None])[None]
    m, l, acc = _merge(carry, _attn_partial(q_abs, q_pe, lat_new.astype(F32), kpe_new.astype(F32), mask))
    return _mla_out(acc / l[..., None], w_uv, w_o, xn.dtype)


def _hier_moe(xn, w_group, b_group, w_expert, b_expert, w_gate, w_up, w_down):
    lead = xn.shape[:-1]
    x = xn.reshape(-1, xn.shape[-1])
    lg = (x @ w_group).astype(F32) + b_group.astype(F32)
    pg = jax.nn.softmax(lg, -1)
    _, g_top = lax.top_k(lg, 1)
    p_sel = jnp.take_along_axis(pg, g_top, 1)
    le_all = jnp.einsum('td,gde->tge', x, w_expert).astype(F32) + b_expert.astype(F32)
    le = jnp.take_along_axis(le_all, g_top[:, :, None], 1)[:, 0]
    v_top, e_top = lax.top_k(le, TOP_K)
    gates = jax.nn.softmax(v_top, -1) * p_sel
    eid = g_top * EXP_PER_GROUP + e_top
    combine = jnp.sum(jax.nn.one_hot(eid, N_EXPERTS, dtype=F32) * gates[..., None], 1)
    hid = jax.nn.silu(jnp.einsum('td,edf->tef', x, w_gate)) * jnp.einsum('td,edf->tef', x, w_up)
    y = jnp.einsum('tef,efd->td', hid * combine[..., None].astype(hid.dtype), w_down)
    return y.reshape(lead + (y.shape[-1],)).astype(xn.dtype)


def _per_layer_embed(h, p, w_proj, g_norm, w_gate):
    gate = jax.nn.sigmoid(_rmsnorm(h, g_norm) @ w_gate)
    return h + (p @ w_proj) * gate


def setup_inputs(seed: int = 0) -> dict:
    key = jax.random.key(seed)
    ks = jax.random.split(key, 64)
    kl = [ks[i] for i in range(64)]
    nrm = lambda shape, scale: jax.random.normal(kl.pop(), shape, F32) * scale
    gain = lambda shape: 1.0 + nrm(shape, 0.05)
    n_pages = PAST_LEN // PAGE_SIZE
    n_pool = (DEC_BATCH * n_pages * 5) // 4
    page_table = jax.random.permutation(kl.pop(), n_pool)[:DEC_BATCH * n_pages]
    page_table = page_table.reshape(DEC_BATCH, n_pages).astype(jnp.int32)
    D = D_MODEL
    return {
        'x_prompt': nrm((BATCH, SEQ, D), 1.0),
        'x_sample': nrm((DEC_BATCH, DEC_SEQ, D), 1.0),
        'state_wkv': nrm((N_A, DEC_BATCH, A_HEADS, A_HEAD, A_HEAD), 0.3),
        'state_shift': nrm((N_A, DEC_BATCH, D), 1.0),
        'cache_latent': nrm((n_pool, PAGE_SIZE, KV_LORA), 1.0),
        'cache_kpe': nrm((n_pool, PAGE_SIZE, QK_ROPE), 1.0),
        'page_table': page_table,
        'p_prompt': nrm((DEPTH, BATCH, SEQ, PLE_DIM), 1.0),
        'p_sample': nrm((DEPTH, DEC_BATCH, DEC_SEQ, PLE_DIM), 1.0),
        'a_norm': gain((N_A, D)),
        'a_mu': jax.random.uniform(kl.pop(), (N_A, 6, D), F32),
        'a_wr': nrm((N_A, D, D), D ** -0.5),
        'a_wk': nrm((N_A, D, D), D ** -0.5),
        'a_wv': nrm((N_A, D, D), D ** -0.5),
        'a_wo': nrm((N_A, D, D), D ** -0.5),
        'a_w0': nrm((N_A, D), 0.5),
        'a_w1': nrm((N_A, D, DECAY_LORA), D ** -0.5),
        'a_w2': nrm((N_A, DECAY_LORA, D), 0.1 * DECAY_LORA ** -0.5),
        'a_a0': nrm((N_A, D), 0.3),
        'a_a1': nrm((N_A, D, AAA_LORA), D ** -0.5),
        'a_a2': nrm((N_A, AAA_LORA, D), 0.1 * AAA_LORA ** -0.5),
        'a_g1': nrm((N_A, D, GATE_LORA), D ** -0.5),
        'a_g2': nrm((N_A, GATE_LORA, D), GATE_LORA ** -0.5),
        'a_kk': 0.85 + nrm((N_A, D), 0.1),
        'a_ka': 1.0 + nrm((N_A, D), 0.1),
        'a_rk': nrm((N_A, A_HEADS, A_HEAD), 0.1),
        'a_lnx_w': gain((N_A, D)),
        'a_lnx_b': nrm((N_A, D), 0.02),
        'kv_norm': gain((D,)),
        'kv_wdown': nrm((D, KV_LORA + QK_ROPE), D ** -0.5),
        'kv_latent_norm': gain((KV_LORA,)),
        'kv_wuk': nrm((KV_LORA, B_HEADS, QK_NOPE), KV_LORA ** -0.5),
        'kv_wuv': nrm((KV_LORA, B_HEADS, V_HEAD), KV_LORA ** -0.5),
        'b_norm': gain((N_B, D)),
        'b_wdq': nrm((N_B, D, Q_LORA), D ** -0.5),
        'b_qnorm': gain((N_B, Q_LORA)),
        'b_wuq': nrm((N_B, Q_LORA, B_HEADS * (QK_NOPE + QK_ROPE)), Q_LORA ** -0.5),
        'b_wo': nrm((N_B, B_HEADS * V_HEAD, D), (B_HEADS * V_HEAD) ** -0.5),
        'f_norm': gain((DEPTH, D)),
        'f_wgroup': nrm((DEPTH, D, N_GROUPS), D ** -0.5),
        'f_bgroup': nrm((DEPTH, N_GROUPS), 0.01),
        'f_wexpert': nrm((DEPTH, N_GROUPS, D, EXP_PER_GROUP), D ** -0.5),
        'f_bexpert': nrm((DEPTH, N_GROUPS, EXP_PER_GROUP), 0.01),
        'f_wgate': nrm((DEPTH, N_EXPERTS, D, D_EXPERT), D ** -0.5),
        'f_wup': nrm((DEPTH, N_EXPERTS, D, D_EXPERT), D ** -0.5),
        'f_wdown': nrm((DEPTH, N_EXPERTS, D_EXPERT, D), D_EXPERT ** -0.5),
        'pl_wproj': nrm((DEPTH, PLE_DIM, D), PLE_DIM ** -0.5),
        'pl_norm': gain((DEPTH, D)),
        'pl_wgate': nrm((DEPTH, D, D), D ** -0.5),
        'final_norm': gain((D,)),
    }


def reference(x_prompt, x_sample, state_wkv, state_shift, cache_latent, cache_kpe, page_table,
              p_prompt, p_sample,
              a_norm, a_mu, a_wr, a_wk, a_wv, a_wo, a_w0, a_w1, a_w2, a_a0, a_a1, a_a2, a_g1, a_g2,
              a_kk, a_ka, a_rk, a_lnx_w, a_lnx_b,
              kv_norm, kv_wdown, kv_latent_norm, kv_wuk, kv_wuv,
              b_norm, b_wdq, b_qnorm, b_wuq, b_wo,
              f_norm, f_wgroup, f_bgroup, f_wexpert, f_bexpert, f_wgate, f_wup, f_wdown,
              pl_wproj, pl_norm, pl_wgate, final_norm):
    Bp, S, D = x_prompt.shape
    past_len = page_table.shape[1] * cache_latent.shape[1]
    pos_p = jnp.arange(S)
    pos_s = past_len + jnp.arange(x_sample.shape[1])
    zero_S = jnp.zeros((Bp, A_HEADS, A_HEAD, A_HEAD), F32)
    zero_shift = jnp.zeros((Bp, D), x_prompt.dtype)
    hp, hs = x_prompt, x_sample
    wkv_p, wkv_s, sh_p, sh_s = [], [], [], []
    lat_p = kpe_p = lat_s = kpe_s = None
    for i in range(DEPTH):
        if i < N_A:
            aw = (a_mu[i], a_wr[i], a_wk[i], a_wv[i], a_wo[i], a_w0[i], a_w1[i], a_w2[i],
                  a_a0[i], a_a1[i], a_a2[i], a_g1[i], a_g2[i], a_kk[i], a_ka[i], a_rk[i],
                  a_lnx_w[i], a_lnx_b[i])
            op, Sp, lp = _rwkv7_time_mix(_rmsnorm(hp, a_norm[i]), zero_shift, zero_S, *aw)
            os_, Ss, ls = _rwkv7_time_mix(_rmsnorm(hs, a_norm[i]), state_shift[i], state_wkv[i], *aw)
            hp, hs = hp + op, hs + os_
            wkv_p.append(Sp); wkv_s.append(Ss); sh_p.append(lp); sh_s.append(ls)
        else:
            j = i - N_A
            bw = (b_wdq[j], b_qnorm[j], b_wuq[j], kv_wuk, kv_wuv, b_wo[j])
            hp = hp + _mla_prompt(_rmsnorm(hp, b_norm[j]), pos_p, lat_p, kpe_p, *bw)
            hs = hs + _mla_decode(_rmsnorm(hs, b_norm[j]), pos_s, lat_s, kpe_s,
                                  cache_latent, cache_kpe, page_table, *bw)
        fw = (f_wgroup[i], f_bgroup[i], f_wexpert[i], f_bexpert[i], f_wgate[i], f_wup[i], f_wdown[i])
        hp = hp + _hier_moe(_rmsnorm(hp, f_norm[i]), *fw)
        hs = hs + _hier_moe(_rmsnorm(hs, f_norm[i]), *fw)
        hp = _per_layer_embed(hp, p_prompt[i], pl_wproj[i], pl_norm[i], pl_wgate[i])
        hs = _per_layer_embed(hs, p_sample[i], pl_wproj[i], pl_norm[i], pl_wgate[i])
        if i == N_A - 1:
            lat_p, kpe_p = _shared_kv(hp, pos_p, kv_norm, kv_wdown, kv_latent_norm)
            lat_s, kpe_s = _shared_kv(hs, pos_s, kv_norm, kv_wdown, kv_latent_norm)
    y_prompt = _rmsnorm(hp, final_norm)
    y_sample = _rmsnorm(hs, final_norm)
    return (y_prompt, y_sample, jnp.stack(wkv_p), jnp.stack(sh_p), lat_p, kpe_p,
            jnp.stack(wkv_s), jnp.stack(sh_s), lat_s, kpe_s)
```

```python
import functools

import jax
import jax.numpy as jnp
from jax import lax
from jax.experimental import pallas as pl
from jax.experimental.pallas import tpu as pltpu

F32 = jnp.float32
BF16 = jnp.bfloat16
I32 = jnp.int32

RMS_EPS = 1e-6
GN_EPS = 64e-5
ROPE_THETA = 10000.0
RWKV_HEAD = 64
LANES = 128
SUBLANES = 8
HEADS_PER_TILE = LANES // RWKV_HEAD
ROW_TILE = 128
EXPERT_TILE = 128
TOP_K = 2
VMEM_LIMIT = 56 * 1024 * 1024


def _cparams(*sem):
    return pltpu.CompilerParams(dimension_semantics=sem, vmem_limit_bytes=VMEM_LIMIT)


def _pick_tile(n, target, mult=16):
    best = None
    for t in range(mult, min(n, target) + 1, mult):
        if n % t == 0:
            best = t
    assert best is not None, (n, target, mult)
    return best


def _rms(x, g):
    return x * lax.rsqrt(jnp.mean(x * x, -1, keepdims=True) + RMS_EPS) * g


def _split_bf16(x):
    hi = x.astype(BF16)
    lo = (x - hi.astype(F32)).astype(BF16)
    return hi, lo


def _seg_ones():
    r = lax.broadcasted_iota(I32, (LANES, LANES), 0) // RWKV_HEAD
    c = lax.broadcasted_iota(I32, (LANES, LANES), 1) // RWKV_HEAD
    return (r == c).astype(BF16)


def _seg_sum(x, seg):
    outs = []
    for j in range(x.shape[1] // LANES):
        hi, lo = _split_bf16(x[:, j * LANES:(j + 1) * LANES])
        outs.append(jnp.dot(hi, seg, preferred_element_type=F32)
                    + jnp.dot(lo, seg, preferred_element_type=F32))
    return jnp.concatenate(outs, axis=1)


def _mm_kernel(*refs, n_extra, n_out, epilogue):
    x_ref, w_ref = refs[0], refs[1]
    extra_refs = refs[2:2 + n_extra]
    out_refs = refs[2 + n_extra:2 + n_extra + n_out]
    wb_ref = refs[2 + n_extra + n_out]

    @pl.when(pl.program_id(1) == 0)
    def _():
        wb_ref[...] = w_ref[...].astype(BF16)

    acc = jnp.dot(x_ref[...].astype(BF16), wb_ref[...], preferred_element_type=F32)
    outs = epilogue(acc, *[r[...] for r in extra_refs]) if epilogue is not None else acc
    if not isinstance(outs, (tuple, list)):
        outs = (outs,)
    for o_ref, o in zip(out_refs, outs):
        o_ref[...] = o.astype(o_ref.dtype)


def _mm(x, w, *, slab=None, epilogue=None, extras=(), out_dtypes=(F32,), out_cols=None, tn=512,
        tm_target=832, name="mm"):
    M, K = x.shape[-2:]
    N = w.shape[1]
    tn = min(tn, N)
    assert N % tn == 0
    tm = _pick_tile(M, tm_target)
    out_cols = out_cols or [tn] * len(out_dtypes)
    if x.ndim == 3:
        x_spec = pl.BlockSpec((None, tm, K), lambda j, i: (slab, i, 0))
    else:
        x_spec = pl.BlockSpec((tm, K), lambda j, i: (i, 0))
    in_specs = [x_spec, pl.BlockSpec((K, tn), lambda j, i: (0, j))]
    args = [x, w]
    for kind, arr in extras:
        if kind == "row":
            in_specs.append(pl.BlockSpec((1, tn), lambda j, i: (0, j)))
        elif kind == "full":
            in_specs.append(pl.BlockSpec((tm, tn), lambda j, i: (i, j)))
        elif kind == "rows":
            in_specs.append(pl.BlockSpec((tm, arr.shape[1]), lambda j, i: (i, 0)))
        elif kind == "const":
            in_specs.append(pl.BlockSpec(arr.shape, lambda j, i: (0,) * arr.ndim))
        else:
            raise ValueError(kind)
        args.append(arr)
    nj = N // tn
    out_shape = [jax.ShapeDtypeStruct((M, oc * nj), dt) for dt, oc in zip(out_dtypes, out_cols)]
    out_specs = [pl.BlockSpec((tm, oc), lambda j, i: (i, j)) for oc in out_cols]
    res = pl.pallas_call(
        functools.partial(_mm_kernel, n_extra=len(extras), n_out=len(out_dtypes), epilogue=epilogue),
        out_shape=out_shape,
        grid=(nj, M // tm),
        in_specs=in_specs,
        out_specs=out_specs,
        scratch_shapes=[pltpu.VMEM((K, tn), BF16)],
        compiler_params=_cparams("parallel", "arbitrary"),
        name=name,
    )(*args)
    return res[0] if len(res) == 1 else res


def _norm_kernel(h_ref, g_ref, o_ref):
    o_ref[...] = _rms(h_ref[...], g_ref[...]).astype(o_ref.dtype)


def _norm(h, g, out_dtype):
    M, D = h.shape
    return pl.pallas_call(
        _norm_kernel,
        out_shape=jax.ShapeDtypeStruct((M, D), out_dtype),
        grid=(M // ROW_TILE,),
        in_specs=[pl.BlockSpec((ROW_TILE, D), lambda i: (i, 0)),
                  pl.BlockSpec((1, D), lambda i: (0, 0))],
        out_specs=pl.BlockSpec((ROW_TILE, D), lambda i: (i, 0)),
        compiler_params=_cparams("parallel"),
        name="rmsnorm",
    )(h, g.reshape(1, D))


def _mix_kernel(h_ref, g_ref, mu_ref, sp_ref, xm_ref, xn_ref, carry_ref, *, tiles_per_seq, n_prompt_tiles):
    i = pl.program_id(0)
    xn = _rms(h_ref[...], g_ref[...])
    xn_ref[...] = xn
    prev = jnp.where(i % tiles_per_seq == 0, 0.0, carry_ref[...])
    row = lax.broadcasted_iota(I32, xn.shape, 0)
    shifted = jnp.where(row == 0, prev, pltpu.roll(xn, 1, 0))
    shifted = jnp.where(i >= n_prompt_tiles, sp_ref[...], shifted)
    carry_ref[...] = xn[ROW_TILE - 1:ROW_TILE, :]
    dx = shifted - xn
    for j in range(mu_ref.shape[0]):
        xm_ref[j] = (xn + dx * mu_ref[j:j + 1, :]).astype(BF16)


def _norm_shift_mix(h, g, mu, state_shift, seq, n_prompt):
    M, D = h.shape
    assert seq % ROW_TILE == 0 and n_prompt % ROW_TILE == 0 and (M - n_prompt) % ROW_TILE == 0
    npt = n_prompt // ROW_TILE
    nmix = mu.shape[0]
    return pl.pallas_call(
        functools.partial(_mix_kernel, tiles_per_seq=seq // ROW_TILE, n_prompt_tiles=npt),
        out_shape=[jax.ShapeDtypeStruct((nmix, M, D), BF16), jax.ShapeDtypeStruct((M, D), F32)],
        grid=(M // ROW_TILE,),
        in_specs=[pl.BlockSpec((ROW_TILE, D), lambda i: (i, 0)),
                  pl.BlockSpec((1, D), lambda i: (0, 0)),
                  pl.BlockSpec((nmix, D), lambda i: (0, 0)),
                  pl.BlockSpec((ROW_TILE, D), lambda i: (jnp.maximum(i - npt, 0), 0))],
        out_specs=[pl.BlockSpec((nmix, ROW_TILE, D), lambda i: (0, i, 0)),
                   pl.BlockSpec((ROW_TILE, D), lambda i: (i, 0))],
        scratch_shapes=[pltpu.VMEM((1, D), F32)],
        compiler_params=_cparams("arbitrary"),
        name="norm_shift_mix",
    )(h, g.reshape(1, D), mu, state_shift)


def _wkv_post_kernel(y_ref, r_ref, k_ref, v_ref, g_ref, lw_ref, lb_ref, rk_ref, o_ref):
    seg = _seg_ones()
    inv = 1.0 / RWKV_HEAD
    y = y_ref[...]
    mean = _seg_sum(y, seg) * inv
    d = y - mean
    var = _seg_sum(d * d, seg) * inv
    yn = d * lax.rsqrt(var + GN_EPS) * lw_ref[...] + lb_ref[...]
    v = v_ref[...]
    bonus = _seg_sum(r_ref[...] * k_ref[...] * rk_ref[...], seg) * v
    o_ref[...] = ((yn + bonus) * g_ref[...]).astype(o_ref.dtype)


def _wkv_post(y, r, k2, v, g, lnx_w, lnx_b, r_k):
    M, D = y.shape
    big = pl.BlockSpec((ROW_TILE, D), lambda i: (i, 0))
    row = pl.BlockSpec((1, D), lambda i: (0, 0))
    return pl.pallas_call(
        _wkv_post_kernel,
        out_shape=jax.ShapeDtypeStruct((M, D), BF16),
        grid=(M // ROW_TILE,),
        in_specs=[big, big, big, big, big, row, row, row],
        out_specs=big,
        compiler_params=_cparams("parallel"),
        name="wkv_post",
    )(y, r, k2, v, g, lnx_w.reshape(1, D), lnx_b.reshape(1, D), r_k.reshape(1, D))


def _bcast_row(x, i, ntile):
    return jnp.concatenate(
        [jnp.broadcast_to(x[i:i + 1, j * LANES:(j + 1) * LANES], (RWKV_HEAD, LANES))
         for j in range(ntile)], axis=0)


def _wkv_consts(ntile):
    seg = _seg_ones()
    rows = ntile * RWKV_HEAD
    rr = lax.broadcasted_iota(I32, (rows, LANES), 0) % RWKV_HEAD
    cc = lax.broadcasted_iota(I32, (rows, LANES), 1) % RWKV_HEAD
    diag = rr == cc
    eh = (lax.broadcasted_iota(I32, (SUBLANES, LANES), 0)
          == lax.broadcasted_iota(I32, (SUBLANES, LANES), 1) // RWKV_HEAD).astype(BF16)
    return seg, diag, eh


def _wkv_step(S, vals, i, ntile, consts):
    r8, w8, k8, v8, a8, b8 = vals
    seg, diag, eh = consts
    a = _bcast_row(a8, i, ntile)
    sa = jnp.dot((S * a).astype(BF16), seg, preferred_element_type=F32)
    v = _bcast_row(v8, i, ntile)
    vb = jnp.dot(jnp.where(diag, v, 0.0).astype(BF16), seg, preferred_element_type=F32)
    s_new = S * _bcast_row(w8, i, ntile) + sa * _bcast_row(b8, i, ntile) + vb * _bcast_row(k8, i, ntile)
    p = (s_new * _bcast_row(r8, i, ntile)).astype(BF16)
    y_t = lax.dot_general(eh, p, (((1,), (1,)), ((), ())), preferred_element_type=F32)
    return s_new, y_t


def _wkv_seq_kernel(*refs, nb, tc, ntile):
    in_refs = refs[:6 * nb]
    s0_ref = refs[6 * nb]
    y_refs = refs[6 * nb + 1:6 * nb + 1 + HEADS_PER_TILE]
    s_ref = refs[6 * nb + 1 + HEADS_PER_TILE]
    consts = _wkv_consts(ntile)

    @pl.when(pl.program_id(1) == 0)
    def _():
        s_ref[...] = s0_ref[...]

    def body(t8, carry):
        base = pl.multiple_of(t8 * SUBLANES, SUBLANES)
        for u in range(nb):
            vals = [ref[pl.ds(base, SUBLANES), :] for ref in in_refs[6 * u:6 * u + 6]]
            S = s_ref[u]
            ys = []
            for i in range(SUBLANES):
                S, y_t = _wkv_step(S, vals, i, ntile, consts)
                ys.append(y_t)
            s_ref[u] = S
            for hs in range(HEADS_PER_TILE):
                y_refs[hs][u, pl.ds(base, SUBLANES), :] = jnp.concatenate([y[hs:hs + 1, :] for y in ys], axis=0)
        return carry

    lax.fori_loop(0, tc // SUBLANES, body, 0)


def _wkv_scan_seq(rows, s0, *, nseq, seq, row0, nb, tc):
    D = rows[0].shape[1]
    ntile = D // LANES
    assert nseq % nb == 0 and seq % tc == 0 and row0 % tc == 0
    nt = seq // tc
    base = row0 // tc
    in_specs, args = [], []
    for u in range(nb):
        for arr in rows:
            in_specs.append(pl.BlockSpec((tc, D), lambda g, c, u=u: (base + (g * nb + u) * nt + c, 0)))
            args.append(arr)
    srows = ntile * RWKV_HEAD
    in_specs.append(pl.BlockSpec((nb, srows, LANES), lambda g, c: (g, 0, 0)))
    args.append(s0)
    half = D // HEADS_PER_TILE
    y_spec = pl.BlockSpec((nb, tc, half), lambda g, c: (g, c, 0))
    res = pl.pallas_call(
        functools.partial(_wkv_seq_kernel, nb=nb, tc=tc, ntile=ntile),
        out_shape=[jax.ShapeDtypeStruct((nseq, seq, half), F32)] * HEADS_PER_TILE
        + [jax.ShapeDtypeStruct((nseq, srows, LANES), F32)],
        grid=(nseq // nb, nt),
        in_specs=in_specs,
        out_specs=[y_spec] * HEADS_PER_TILE + [pl.BlockSpec((nb, srows, LANES), lambda g, c: (g, 0, 0))],
        compiler_params=_cparams("parallel", "arbitrary"),
        name="wkv_scan_seq",
    )(*args)
    return res[:HEADS_PER_TILE], res[HEADS_PER_TILE]


def _wkv_one_kernel(*refs, nu, ntile):
    in_refs = refs[:6]
    s0_ref = refs[6]
    y_refs = refs[7:7 + HEADS_PER_TILE]
    s_ref = refs[7 + HEADS_PER_TILE]
    consts = _wkv_consts(ntile)
    vals = [ref[...] for ref in in_refs]
    ys = []
    for u in range(nu):
        s_new, y_t = _wkv_step(s0_ref[u], vals, u, ntile, consts)
        s_ref[u] = s_new
        ys.append(y_t)
    for hs in range(HEADS_PER_TILE):
        y_refs[hs][...] = jnp.concatenate([y[hs:hs + 1, :] for y in ys], axis=0)


def _wkv_scan_one(rows, s0, *, nseq, row0, nu=SUBLANES):
    D = rows[0].shape[1]
    ntile = D // LANES
    assert nseq % nu == 0 and row0 % nu == 0
    base = row0 // nu
    srows = ntile * RWKV_HEAD
    half = D // HEADS_PER_TILE
    in_specs = [pl.BlockSpec((nu, D), lambda g: (base + g, 0)) for _ in rows]
    in_specs.append(pl.BlockSpec((nu, srows, LANES), lambda g: (g, 0, 0)))
    y_spec = pl.BlockSpec((nu, half), lambda g: (g, 0))
    res = pl.pallas_call(
        functools.partial(_wkv_one_kernel, nu=nu, ntile=ntile),
        out_shape=[jax.ShapeDtypeStruct((nseq, half), F32)] * HEADS_PER_TILE
        + [jax.ShapeDtypeStruct((nseq, srows, LANES), F32)],
        grid=(nseq // nu,),
        in_specs=in_specs,
        out_specs=[y_spec] * HEADS_PER_TILE + [pl.BlockSpec((nu, srows, LANES), lambda g: (g, 0, 0))],
        compiler_params=_cparams("parallel"),
        name="wkv_scan_one",
    )(*rows, s0)
    return res[:HEADS_PER_TILE], res[HEADS_PER_TILE]


def _state_to_tiles(s):
    n, H = s.shape[0], s.shape[1]
    s = s.reshape(n, H // HEADS_PER_TILE, HEADS_PER_TILE, RWKV_HEAD, RWKV_HEAD)
    return jnp.transpose(s, (0, 1, 3, 2, 4)).reshape(n, H // HEADS_PER_TILE * RWKV_HEAD, LANES)


def _tiles_to_state(s, H):
    n = s.shape[0]
    s = s.reshape(n, H // HEADS_PER_TILE, RWKV_HEAD, HEADS_PER_TILE, RWKV_HEAD)
    return jnp.transpose(s, (0, 1, 3, 2, 4)).reshape(n, H, RWKV_HEAD, RWKV_HEAD)


def _y_halves_to_rows(halves, H):
    lead = halves[0].shape[:-1]
    st = jnp.stack([x.reshape(lead + (H // HEADS_PER_TILE, RWKV_HEAD)) for x in halves], axis=-2)
    return st.reshape(lead + (H * RWKV_HEAD,))


def _router_kernel(h_ref, g_ref, w_ref, b_ref, gid_ref, xb_ref, info_ref, wh_ref, wl_ref, *, n_groups):
    @pl.when(pl.program_id(0) == 0)
    def _():
        wh, wl = _split_bf16(w_ref[...])
        wh_ref[...] = wh
        wl_ref[...] = wl

    xn = _rms(h_ref[...], g_ref[...])
    xh, xl = _split_bf16(xn)
    xb_ref[...] = xh
    logits = (jnp.dot(xh, wh_ref[...], preferred_element_type=F32)
              + jnp.dot(xl, wh_ref[...], preferred_element_type=F32)
              + jnp.dot(xh, wl_ref[...], preferred_element_type=F32)) + b_ref[...]
    lane = lax.broadcasted_iota(I32, logits.shape, 1)
    lanef = lane.astype(F32)
    neg = -jnp.inf
    big = float(LANES)
    lg = jnp.where(lane < n_groups, logits, neg)
    mg = jnp.max(lg, -1, keepdims=True)
    g_top = jnp.min(jnp.where(lg == mg, lanef, big), -1, keepdims=True)
    p_sel = 1.0 / jnp.sum(jnp.exp(lg - mg), -1, keepdims=True)
    le = jnp.where(gid_ref[...] == g_top, logits, neg)
    v1 = jnp.max(le, -1, keepdims=True)
    i1 = jnp.min(jnp.where(le == v1, lanef, big), -1, keepdims=True)
    le2 = jnp.where(lanef == i1, neg, le)
    v2 = jnp.max(le2, -1, keepdims=True)
    i2 = jnp.min(jnp.where(le2 == v2, lanef, big), -1, keepdims=True)
    e2 = jnp.exp(v2 - v1)
    den = 1.0 + e2
    gate1 = (1.0 / den) * p_sel
    gate2 = (e2 / den) * p_sel
    info = jnp.where(lane == 0, i1 - n_groups,
                     jnp.where(lane == 1, i2 - n_groups,
                               jnp.where(lane == 2, gate1, jnp.where(lane == 3, gate2, 0.0))))
    info_ref[...] = info


def _router(h, g, w_route, b_route, n_groups, per_group):
    M, D = h.shape
    lane = jnp.arange(LANES, dtype=I32)
    is_expert = (lane >= n_groups) & (lane < n_groups * (1 + per_group))
    gid = jnp.where(is_expert, (lane - n_groups) // per_group, -1).astype(F32).reshape(1, LANES)
    return pl.pallas_call(
        functools.partial(_router_kernel, n_groups=n_groups),
        out_shape=[jax.ShapeDtypeStruct((M, D), BF16), jax.ShapeDtypeStruct((M, LANES), F32)],
        grid=(M // ROW_TILE,),
        in_specs=[pl.BlockSpec((ROW_TILE, D), lambda i: (i, 0)),
                  pl.BlockSpec((1, D), lambda i: (0, 0)),
                  pl.BlockSpec((D, LANES), lambda i: (0, 0)),
                  pl.BlockSpec((1, LANES), lambda i: (0, 0)),
                  pl.BlockSpec((1, LANES), lambda i: (0, 0))],
        out_specs=[pl.BlockSpec((ROW_TILE, D), lambda i: (i, 0)),
                   pl.BlockSpec((ROW_TILE, LANES), lambda i: (i, 0))],
        scratch_shapes=[pltpu.VMEM((D, LANES), BF16), pltpu.VMEM((D, LANES), BF16)],
        compiler_params=_cparams("arbitrary"),
        name="moe_router",
    )(h, g.reshape(1, D), w_route, b_route, gid)


def _expert_kernel(te_ref, used_ref, x_ref, gate_ref, wg_ref, wu_ref, wd_ref, o_ref, wgb, wub, wdb):
    i = pl.program_id(0)
    new_expert = jnp.logical_or(i == 0, te_ref[i] != te_ref[jnp.maximum(i - 1, 0)])

    @pl.when(jnp.logical_and(i < used_ref[0], new_expert))
    def _():
        wgb[...] = wg_ref[0].astype(BF16)
        wub[...] = wu_ref[0].astype(BF16)
        wdb[...] = wd_ref[0].astype(BF16)

    @pl.when(i < used_ref[0])
    def _():
        x = x_ref[...]
        hg = jnp.dot(x, wgb[...], preferred_element_type=F32)
        hu = jnp.dot(x, wub[...], preferred_element_type=F32)
        hid = (hg * jax.nn.sigmoid(hg)) * hu * gate_ref[...]
        o_ref[...] = jnp.dot(hid.astype(BF16), wdb[...], preferred_element_type=F32)

    @pl.when(i >= used_ref[0])
    def _():
        o_ref[...] = jnp.zeros_like(o_ref)


def _experts(tile_expert, used, xs, row_gate, w_gate, w_up, w_down):
    R, D = xs.shape
    E, _, Fd = w_gate.shape
    nt = R // EXPERT_TILE
    return pl.pallas_call(
        _expert_kernel,
        out_shape=jax.ShapeDtypeStruct((R, D), F32),
        grid_spec=pltpu.PrefetchScalarGridSpec(
            num_scalar_prefetch=2,
            grid=(nt,),
            in_specs=[pl.BlockSpec((EXPERT_TILE, D), lambda i, te, u: (i, 0)),
                      pl.BlockSpec((EXPERT_TILE, 1), lambda i, te, u: (i, 0)),
                      pl.BlockSpec((1, D, Fd), lambda i, te, u: (te[i], 0, 0)),
                      pl.BlockSpec((1, D, Fd), lambda i, te, u: (te[i], 0, 0)),
                      pl.BlockSpec((1, Fd, D), lambda i, te, u: (te[i], 0, 0))],
            out_specs=pl.BlockSpec((EXPERT_TILE, D), lambda i, te, u: (i, 0)),
            scratch_shapes=[pltpu.VMEM((D, Fd), BF16), pltpu.VMEM((D, Fd), BF16), pltpu.VMEM((Fd, D), BF16)],
        ),
        compiler_params=_cparams("arbitrary"),
        name="moe_experts",
    )(tile_expert, used, xs, row_gate, w_gate, w_up, w_down)


def _moe(h, f_norm, w_group, b_group, w_expert, b_expert, w_gate, w_up, w_down):
    M, D = h.shape
    G = w_group.shape[1]
    P = w_expert.shape[2]
    E = G * P
    assert G + E <= LANES
    w_route = jnp.zeros((D, LANES), F32)
    w_route = w_route.at[:, :G].set(w_group)
    w_route = w_route.at[:, G:G + E].set(jnp.transpose(w_expert, (1, 0, 2)).reshape(D, E))
    b_route = jnp.zeros((1, LANES), F32)
    b_route = b_route.at[0, :G].set(b_group)
    b_route = b_route.at[0, G:G + E].set(b_expert.reshape(E))
    xb, info = _router(h, f_norm, w_route, b_route, G, P)

    eid = info[:, :TOP_K].astype(I32).reshape(-1)
    gates = info[:, TOP_K:2 * TOP_K].reshape(-1)
    npair = M * TOP_K
    nt = -(-npair // EXPERT_TILE) + E
    R = nt * EXPERT_TILE
    order = jnp.argsort(eid, stable=True)
    counts = jnp.zeros((E,), I32).at[eid].add(1)
    padded = (counts + EXPERT_TILE - 1) // EXPERT_TILE * EXPERT_TILE
    pend = jnp.cumsum(padded)
    pstart = pend - padded
    start = jnp.cumsum(counts) - counts
    e_sorted = eid[order]
    dest = pstart[e_sorted] + (jnp.arange(npair, dtype=I32) - start[e_sorted])
    row_token = jnp.zeros((R,), I32).at[dest].set((order // TOP_K).astype(I32))
    row_gate = jnp.zeros((R,), F32).at[dest].set(gates[order])
    pos = jnp.zeros((npair,), I32).at[order].set(dest)
    tile_expert = jnp.minimum(
        jnp.searchsorted(pend, jnp.arange(nt, dtype=I32) * EXPERT_TILE, side="right"), E - 1).astype(I32)
    used = (pend[-1] // EXPERT_TILE).astype(I32).reshape(1)

    xs = jnp.take(xb, row_token, axis=0)
    ys = _experts(tile_expert, used, xs, row_gate.reshape(R, 1), w_gate, w_up, w_down)
    pos = pos.reshape(M, TOP_K)
    y = jnp.take(ys, pos[:, 0], axis=0)
    for s in range(1, TOP_K):
        y = y + jnp.take(ys, pos[:, s], axis=0)
    return h + y


def _attn_prompt_kernel(qn_ref, qp_ref, kv_ref, wuk_ref, wuv_ref, o_ref, q_s, m_s, l_s, acc_s,
                        *, tq, tk, nh, dn, dc, scale):
    qi = pl.program_id(1)
    kv = pl.program_id(2)
    last = (qi * tq + tq - 1) // tk
    rows = nh * tq

    @pl.when(kv == 0)
    def _():
        for h in range(nh):
            qa = jnp.dot(qn_ref[:, h * dn:(h + 1) * dn], wuk_ref[h], preferred_element_type=F32) * scale
            q_s[h * tq:(h + 1) * tq, 0:dc] = qa.astype(BF16)
            q_s[h * tq:(h + 1) * tq, dc:] = qp_ref[:, h * LANES:(h + 1) * LANES]
        m_s[...] = jnp.full(m_s.shape, -jnp.inf, F32)
        l_s[...] = jnp.zeros(l_s.shape, F32)
        acc_s[...] = jnp.zeros(acc_s.shape, F32)

    def update(masked):
        kblk = kv_ref[...]
        s = lax.dot_general(q_s[...], kblk, (((1,), (1,)), ((), ())), preferred_element_type=F32)
        if masked:
            qpos = qi * tq + lax.broadcasted_iota(I32, (rows, tk), 0) % tq
            kpos = kv * tk + lax.broadcasted_iota(I32, (rows, tk), 1)
            s = jnp.where(kpos <= qpos, s, -jnp.inf)
        m_old = m_s[...]
        m_new = jnp.maximum(m_old, jnp.max(s, -1, keepdims=True))
        alpha = jnp.exp(m_old - m_new)
        p = jnp.exp(s - m_new)
        l_s[...] = alpha * l_s[...] + jnp.sum(p, -1, keepdims=True)
        acc_s[...] = alpha * acc_s[...] + jnp.dot(p.astype(BF16), kblk[:, 0:dc], preferred_element_type=F32)
        m_s[...] = m_new

    @pl.when(kv < last)
    def _():
        update(False)

    @pl.when(kv == last)
    def _():
        update(True)
        o = acc_s[...] / l_s[...]
        for h in range(nh):
            oh = jnp.dot(o[h * tq:(h + 1) * tq].astype(BF16), wuv_ref[h], preferred_element_type=F32)
            o_ref[:, h * oh.shape[1]:(h + 1) * oh.shape[1]] = oh.astype(o_ref.dtype)


def _attn_prompt(qn, qp, kvb, wuk_t, wuv_t, *, nb, seq, tq, tk, scale):
    nh, dn, dc = wuk_t.shape
    dv = wuv_t.shape[2]
    width = kvb.shape[1]
    assert width == dc + LANES and qp.shape[1] == nh * LANES
    nq, nk = seq // tq, seq // tk

    def kv_map(b, qi, kv):
        return (b * nk + jnp.minimum(kv, (qi * tq + tq - 1) // tk), 0)

    return pl.pallas_call(
        functools.partial(_attn_prompt_kernel, tq=tq, tk=tk, nh=nh, dn=dn, dc=dc, scale=scale),
        out_shape=jax.ShapeDtypeStruct((nb * seq, nh * dv), BF16),
        grid=(nb, nq, nk),
        in_specs=[pl.BlockSpec((tq, nh * dn), lambda b, qi, kv: (b * nq + qi, 0)),
                  pl.BlockSpec((tq, nh * LANES), lambda b, qi, kv: (b * nq + qi, 0)),
                  pl.BlockSpec((tk, width), kv_map),
                  pl.BlockSpec((nh, dn, dc), lambda b, qi, kv: (0, 0, 0)),
                  pl.BlockSpec((nh, dc, dv), lambda b, qi, kv: (0, 0, 0))],
        out_specs=pl.BlockSpec((tq, nh * dv), lambda b, qi, kv: (b * nq + qi, 0)),
        scratch_shapes=[pltpu.VMEM((nh * tq, width), BF16),
                        pltpu.VMEM((nh * tq, 1), F32),
                        pltpu.VMEM((nh * tq, 1), F32),
                        pltpu.VMEM((nh * tq, dc), F32)],
        compiler_params=_cparams("parallel", "parallel", "arbitrary"),
        name="mla_prompt_attn",
    )(qn, qp, kvb, wuk_t, wuv_t)


def _attn_decode_kernel(pt_ref, *refs, npg, dr, dc):
    qa_ref, qr_ref, kvn_ref = refs[:3]
    lat_refs = refs[3:3 + npg]
    kpe_refs = refs[3 + npg:3 + 2 * npg]
    o_ref = refs[3 + 2 * npg]
    m_s, l_s, acc_s = refs[4 + 2 * npg:]
    g = pl.program_id(1)

    @pl.when(g == 0)
    def _():
        m_s[...] = jnp.full(m_s.shape, -jnp.inf, F32)
        l_s[...] = jnp.zeros(l_s.shape, F32)
        acc_s[...] = jnp.zeros(acc_s.shape, F32)

    qa = qa_ref[0]
    qr = qr_ref[0][:, 0:dr]
    nt = (((1,), (1,)), ((), ()))
    for i in range(npg):
        lat = lat_refs[i][0].astype(BF16)
        kpe = kpe_refs[i][0].astype(BF16)
        s = (lax.dot_general(qa, lat, nt, preferred_element_type=F32)
             + lax.dot_general(qr, kpe, nt, preferred_element_type=F32))
        m_old = m_s[...]
        m_new = jnp.maximum(m_old, jnp.max(s, -1, keepdims=True))
        alpha = jnp.exp(m_old - m_new)
        p = jnp.exp(s - m_new)
        l_s[...] = alpha * l_s[...] + jnp.sum(p, -1, keepdims=True)
        acc_s[...] = alpha * acc_s[...] + jnp.dot(p.astype(BF16), lat, preferred_element_type=F32)
        m_s[...] = m_new

    @pl.when(g == pl.num_programs(1) - 1)
    def _():
        kvn = kvn_ref[0].astype(F32)
        s = (jnp.sum(qa.astype(F32) * kvn[:, 0:dc], -1, keepdims=True)
             + jnp.sum(qr.astype(F32) * kvn[:, dc:dc + dr], -1, keepdims=True))
        m_old = m_s[...]
        m_new = jnp.maximum(m_old, s)
        alpha = jnp.exp(m_old - m_new)
        p = jnp.exp(s - m_new)
        l = alpha * l_s[...] + p
        acc = alpha * acc_s[...] + p * kvn[:, 0:dc]
        o_ref[0] = (acc / l).astype(o_ref.dtype)


def _attn_decode(qa, qr, kvn, cache_latent, cache_kpe, page_table, *, dr, npg):
    nseq, npages = page_table.shape
    _, nh, dc = qa.shape
    width = kvn.shape[2]
    page = cache_latent.shape[1]
    assert npages % npg == 0

    def seq_spec(a, b):
        return pl.BlockSpec((1, a, b), lambda s, g, pt: (s, 0, 0))

    in_specs = [seq_spec(nh, dc), seq_spec(nh, LANES), seq_spec(1, width)]
    for i in range(npg):
        in_specs.append(pl.BlockSpec((1, page, dc), lambda s, g, pt, i=i: (pt[s * npages + g * npg + i], 0, 0)))
    for i in range(npg):
        in_specs.append(pl.BlockSpec((1, page, dr), lambda s, g, pt, i=i: (pt[s * npages + g * npg + i], 0, 0)))
    return pl.pallas_call(
        functools.partial(_attn_decode_kernel, npg=npg, dr=dr, dc=dc),
        out_shape=jax.ShapeDtypeStruct((nseq, nh, dc), BF16),
        grid_spec=pltpu.PrefetchScalarGridSpec(
            num_scalar_prefetch=1,
            grid=(nseq, npages // npg),
            in_specs=in_specs,
            out_specs=seq_spec(nh, dc),
            scratch_shapes=[pltpu.VMEM((nh, 1), F32), pltpu.VMEM((nh, 1), F32), pltpu.VMEM((nh, dc), F32)],
        ),
        compiler_params=_cparams("parallel", "arbitrary"),
        name="mla_decode_attn",
    )(page_table.reshape(-1), qa, qr, kvn, *([cache_latent] * npg), *([cache_kpe] * npg))


def _head_mm_kernel(x_ref, w_ref, o_ref, *, scale):
    acc = jnp.dot(x_ref[...], w_ref[0], preferred_element_type=F32)
    if scale is not None:
        acc = acc * scale
    o_ref[...] = acc.astype(o_ref.dtype)


def _head_mm(x, w, scale=None):
    R = x.shape[0]
    nh, kin, kout = w.shape
    return pl.pallas_call(
        functools.partial(_head_mm_kernel, scale=scale),
        out_shape=jax.ShapeDtypeStruct((R, nh * kout), BF16),
        grid=(nh,),
        in_specs=[pl.BlockSpec((R, kin), lambda h: (0, h)),
                  pl.BlockSpec((1, kin, kout), lambda h: (h, 0, 0))],
        out_specs=pl.BlockSpec((R, kout), lambda h: (0, h)),
        compiler_params=_cparams("parallel"),
        name="head_mm",
    )(x, w)


def _rope_tables(pos, dr):
    half = dr // 2
    inv = ROPE_THETA ** (-2.0 * jnp.arange(half, dtype=F32) / dr)
    ang = pos.astype(F32)[:, None] * inv[None, :]
    cos, sin = jnp.cos(ang), jnp.sin(ang)
    rep = LANES // dr
    cos_t = jnp.tile(jnp.concatenate([cos, cos], -1), (1, rep))
    sin_t = jnp.tile(jnp.concatenate([-sin, sin], -1), (1, rep))
    return cos_t, sin_t


def _swap_halves(w, dr):
    return jnp.concatenate([w[..., dr // 2:], w[..., :dr // 2]], -1)


def _rwkv_layer(h, state_wkv, state_shift, n_prompt, nb, seq, aw):
    (a_norm, mu, w_r, w_k, w_v, w_o, w0, w1, w2, a0, a1, a2, g1, g2, k_k, k_a, r_k, lnx_w, lnx_b) = aw
    M, D = h.shape
    H = D // RWKV_HEAD
    nsample = M - n_prompt
    row = lambda x: ("row", x.reshape(1, D))
    xm, xn = _norm_shift_mix(h, a_norm, mu, state_shift, seq, n_prompt)
    jr, jw, jk, jv, ja, jg = range(6)

    r = _mm(xm, w_r, slab=jr, name="rwkv_r")
    v = _mm(xm, w_v, slab=jv, name="rwkv_v")
    th = _mm(xm, w1, slab=jw, epilogue=jnp.tanh, out_dtypes=(BF16,), name="rwkv_w1")
    def decay_epilogue(z, w0r):
        u = -(w0r + z)
        softplus = jnp.maximum(u, 0.0) + jnp.log(1.0 + jnp.exp(-jnp.abs(u)))
        return jnp.exp(-jnp.exp(-softplus - 0.5))

    decay = _mm(th, w2, extras=[row(w0)], epilogue=decay_epilogue, name="rwkv_w2")
    al = _mm(xm, a1, slab=ja, out_dtypes=(BF16,), name="rwkv_a1")
    a_lr = _mm(al, a2, extras=[row(a0)], name="rwkv_a2", epilogue=lambda z, a0r: jax.nn.sigmoid(a0r + z))
    gl = _mm(xm, g1, slab=jg, epilogue=jax.nn.sigmoid, out_dtypes=(BF16,), name="rwkv_g1")
    g = _mm(gl, g2, name="rwkv_g2")

    def k_epilogue(k, a, kkr, kar):
        seg = _seg_ones()
        kk = k * kkr
        nrm = jnp.sqrt(_seg_sum(kk * kk, seg))
        kk = kk / jnp.maximum(nrm, 1e-12)
        return k * (1.0 + (a - 1.0) * kar), -kk, kk * a

    k2, a_neg, b_pos = _mm(xm, w_k, slab=jk, extras=[("full", a_lr), row(k_k), row(k_a)], epilogue=k_epilogue,
                           out_dtypes=(F32, F32, F32), name="rwkv_k")

    rows = (r, decay, k2, v, a_neg, b_pos)
    s0_p = jnp.zeros((nb, H // HEADS_PER_TILE * RWKV_HEAD, LANES), F32)
    yh_p, s_p = _wkv_scan_seq(rows, s0_p, nseq=nb, seq=seq, row0=0, nb=min(nb, 2), tc=min(seq, 64))
    yh_s, s_s = _wkv_scan_one(rows, _state_to_tiles(state_wkv), nseq=nsample, row0=n_prompt)
    y = jnp.concatenate([_y_halves_to_rows(yh_p, H).reshape(n_prompt, D), _y_halves_to_rows(yh_s, H)], 0)

    yo = _wkv_post(y, r, k2, v, g, lnx_w, lnx_b, r_k.reshape(D))
    h = _mm(yo, w_o, extras=[("full", h)], epilogue=lambda acc, hh: hh + acc, name="rwkv_o")
    return h, _tiles_to_state(s_p, H), _tiles_to_state(s_s, H), xn


def _ple_layer(h, p, w_proj, g_norm, w_gate):
    M, D = h.shape
    hn = _norm(h, g_norm, BF16)
    pp = _mm(p, w_proj, name="ple_proj")
    return _mm(hn, w_gate, extras=[("full", h), ("full", pp)], name="ple_gate",
               epilogue=lambda acc, hh, ppp: hh + ppp * jax.nn.sigmoid(acc))


def _shared_kv(h, pos, g_in, w_down, g_latent, dc, dr):
    M, D = h.shape
    cos_t, sin_t = _rope_tables(pos, dr)
    w_ext = jnp.concatenate([w_down, _swap_halves(w_down[:, dc:dc + dr], dr)], 1)
    hn = _norm(h, g_in, BF16)

    assert 2 * dr == LANES

    def epilogue(acc, gl, ct, st):
        lat = _rms(acc[:, :dc], gl)
        t = acc[:, dc:]
        kpe = t * ct + pltpu.roll(t, dr, 1) * st
        out = jnp.concatenate([lat, kpe], 1)
        return out, out

    return _mm(hn, w_ext, extras=[("const", g_latent.reshape(1, dc)), ("rows", cos_t), ("rows", sin_t)],
               epilogue=epilogue, out_dtypes=(F32, BF16), tn=dc + 2 * dr, name="kv_down")


def _mla_layer(h, pos, c_bf, n_prompt, nb, seq, cache_latent, cache_kpe, page_table, bw):
    b_norm, w_dq, g_q, w_uq, w_uk, w_uv, w_o = bw
    M, D = h.shape
    dc, nh, dn = w_uk.shape
    dv = w_uv.shape[2]
    ql = w_uq.shape[0]
    dr = w_uq.shape[1] // nh - dn
    scale = float(dn + dr) ** -0.5
    w_uq3 = w_uq.reshape(ql, nh, dn + dr)
    w_qn = w_uq3[:, :, :dn].reshape(ql, nh * dn)
    w_qr = w_uq3[:, :, dn:]
    assert 2 * dr == LANES
    w_qr_ext = jnp.concatenate([w_qr, _swap_halves(w_qr, dr)], -1).reshape(ql, nh * LANES)
    wuk_t = jnp.transpose(w_uk, (1, 2, 0)).astype(BF16)
    wuv_t = jnp.transpose(w_uv, (1, 0, 2)).astype(BF16)
    cos_t, sin_t = _rope_tables(pos, dr)

    xq = _norm(h, b_norm, BF16)
    cq = _mm(xq, w_dq, extras=[("const", g_q.reshape(1, ql))], epilogue=_rms, out_dtypes=(BF16,),
             tn=ql, name="mla_dq")
    qn = _mm(cq, w_qn, out_dtypes=(BF16,), name="mla_uq_nope")

    def rope_epilogue(acc, ct, st):
        n = acc.shape[1]
        rot = acc * jnp.tile(ct, (1, nh)) + pltpu.roll(acc, n - dr, 1) * jnp.tile(st, (1, nh))
        keep = lax.broadcasted_iota(I32, acc.shape, 1) % LANES < dr
        return jnp.where(keep, rot * scale, 0.0)

    qp = _mm(cq, w_qr_ext, extras=[("rows", cos_t), ("rows", sin_t)], epilogue=rope_epilogue,
             out_dtypes=(BF16,), tn=nh * LANES, name="mla_uq_rope")

    o_p = _attn_prompt(qn, qp, c_bf, wuk_t, wuv_t, nb=nb, seq=seq, tq=min(seq, 128), tk=min(seq, 512),
                       scale=scale)
    ns = M - n_prompt
    qa_s = _head_mm(qn[n_prompt:], wuk_t, scale=scale).reshape(ns, nh, dc)
    ol_s = _attn_decode(qa_s, qp[n_prompt:].reshape(ns, nh, LANES), c_bf[n_prompt:].reshape(ns, 1, -1),
                        cache_latent, cache_kpe, page_table, dr=dr, npg=min(page_table.shape[1], 8))
    o_s = _head_mm(ol_s.reshape(ns, nh * dc), wuv_t)
    o = jnp.concatenate([o_p, o_s], 0)
    return _mm(o, w_o, extras=[("full", h)], epilogue=lambda acc, hh: hh + acc, name="mla_o")


def kernel(x_prompt, x_sample, state_wkv, state_shift, cache_latent, cache_kpe, page_table, p_prompt, p_sample, a_norm, a_mu, a_wr, a_wk, a_wv, a_wo, a_w0, a_w1, a_w2, a_a0, a_a1, a_a2, a_g1, a_g2, a_kk, a_ka, a_rk, a_lnx_w, a_lnx_b, kv_norm, kv_wdown, kv_latent_norm, kv_wuk, kv_wuv, b_norm, b_wdq, b_qnorm, b_wuq, b_wo, f_norm, f_wgroup, f_bgroup, f_wexpert, f_bexpert, f_wgate, f_wup, f_wdown, pl_wproj, pl_norm, pl_wgate, final_norm):
    nb, seq, D = x_prompt.shape
    ns, dec_seq, _ = x_sample.shape
    assert dec_seq == 1
    depth = f_norm.shape[0]
    n_a = a_norm.shape[0]
    n_prompt = nb * seq
    dc = kv_latent_norm.shape[0]
    dr = kv_wdown.shape[1] - dc
    past_len = page_table.shape[1] * cache_latent.shape[1]
    pos = jnp.concatenate([jnp.tile(jnp.arange(seq, dtype=I32), nb), jnp.full((ns,), past_len, I32)])

    h = jnp.concatenate([x_prompt.reshape(n_prompt, D), x_sample.reshape(ns, D)], 0)
    p_all = jnp.concatenate([p_prompt.reshape(depth, n_prompt, -1), p_sample.reshape(depth, ns, -1)], 1)
    wkv_p, wkv_s, sh_p, sh_s = [], [], [], []
    c_f32 = c_bf = None
    for i in range(depth):
        if i < n_a:
            aw = (a_norm[i], a_mu[i], a_wr[i], a_wk[i], a_wv[i], a_wo[i], a_w0[i], a_w1[i], a_w2[i],
                  a_a0[i], a_a1[i], a_a2[i], a_g1[i], a_g2[i], a_kk[i], a_ka[i], a_rk[i],
                  a_lnx_w[i], a_lnx_b[i])
            h, s_p, s_s, xn = _rwkv_layer(h, state_wkv[i], state_shift[i], n_prompt, nb, seq, aw)
            wkv_p.append(s_p)
            wkv_s.append(s_s)
            sh_p.append(xn[:n_prompt].reshape(nb, seq, D)[:, -1])
            sh_s.append(xn[n_prompt:])
        else:
            j = i - n_a
            bw = (b_norm[j], b_wdq[j], b_qnorm[j], b_wuq[j], kv_wuk, kv_wuv, b_wo[j])
            h = _mla_layer(h, pos, c_bf, n_prompt, nb, seq, cache_latent, cache_kpe, page_table, bw)
        h = _moe(h, f_norm[i], f_wgroup[i], f_bgroup[i], f_wexpert[i], f_bexpert[i],
                 f_wgate[i], f_wup[i], f_wdown[i])
        h = _ple_layer(h, p_all[i], pl_wproj[i], pl_norm[i], pl_wgate[i])
        if i == n_a - 1:
            c_f32, c_bf = _shared_kv(h, pos, kv_norm, kv_wdown, kv_latent_norm, dc, dr)
    y = _norm(h, final_norm, F32)
    lat, kpe = c_f32[:, :dc], c_f32[:, dc:dc + dr]
    return (y[:n_prompt].reshape(nb, seq, D), y[n_prompt:].reshape(ns, 1, D),
            jnp.stack(wkv_p), jnp.stack(sh_p),
            lat[:n_prompt].reshape(nb, seq, dc), kpe[:n_prompt].reshape(nb, seq, dr),
            jnp.stack(wkv_s), jnp.stack(sh_s),
            lat[n_prompt:].reshape(ns, 1, dc), kpe[n_prompt:].reshape(ns, 1, dr))
```

```python
import functools

import jax
import jax.numpy as jnp
from jax import lax
from jax.experimental import pallas as pl
from jax.experimental.pallas import tpu as pltpu

F32 = jnp.float32
BF16 = jnp.bfloat16
I32 = jnp.int32

RMS_EPS = 1e-6
GN_EPS = 64e-5
ROPE_THETA = 10000.0
RWKV_HEAD = 64
LANES = 128
SUBLANES = 8
STATE_LANES = 256
HEADS_PER_TILE = STATE_LANES // RWKV_HEAD
ROW_TILE = 128
EXPERT_TILE = 128
TOP_K = 2
VMEM_LIMIT = 56 * 1024 * 1024


def _cparams(*sem):
    return pltpu.CompilerParams(dimension_semantics=sem, vmem_limit_bytes=VMEM_LIMIT)


def _pick_tile(n, target, mult=16):
    best = None
    for t in range(mult, min(n, target) + 1, mult):
        if n % t == 0:
            best = t
    assert best is not None, (n, target, mult)
    return best


def _rms(x, g):
    return x * lax.rsqrt(jnp.mean(x * x, -1, keepdims=True) + RMS_EPS) * g


def _split_bf16(x):
    hi = x.astype(BF16)
    lo = (x - hi.astype(F32)).astype(BF16)
    return hi, lo


def _seg_ones(width=LANES):
    r = lax.broadcasted_iota(I32, (width, width), 0) // RWKV_HEAD
    c = lax.broadcasted_iota(I32, (width, width), 1) // RWKV_HEAD
    return (r == c).astype(BF16)


def _seg_sum(x, seg):
    outs = []
    for j in range(x.shape[1] // LANES):
        hi, lo = _split_bf16(x[:, j * LANES:(j + 1) * LANES])
        outs.append(jnp.dot(hi, seg, preferred_element_type=F32)
                    + jnp.dot(lo, seg, preferred_element_type=F32))
    return jnp.concatenate(outs, axis=1)


def _mm_kernel(*refs, n_extra, n_out, epilogue):
    x_ref, w_ref = refs[0], refs[1]
    extra_refs = refs[2:2 + n_extra]
    out_refs = refs[2 + n_extra:2 + n_extra + n_out]
    wb_ref = refs[2 + n_extra + n_out]

    @pl.when(pl.program_id(1) == 0)
    def _():
        wb_ref[...] = w_ref[...].astype(BF16)

    acc = jnp.dot(x_ref[...].astype(BF16), wb_ref[...], preferred_element_type=F32)
    outs = epilogue(acc, *[r[...] for r in extra_refs]) if epilogue is not None else acc
    if not isinstance(outs, (tuple, list)):
        outs = (outs,)
    for o_ref, o in zip(out_refs, outs):
        o_ref[...] = o.astype(o_ref.dtype)


def _mm(x, w, *, slab=None, layer=None, epilogue=None, extras=(), out_dtypes=(F32,), out_cols=None,
        tn=512, tm_target=832, name="mm"):
    M, K = x.shape[-2:]
    N = w.shape[-1]
    tn = min(tn, N)
    assert N % tn == 0
    tm = _pick_tile(M, tm_target)
    out_cols = out_cols or [tn] * len(out_dtypes)
    if x.ndim == 3:
        x_spec = pl.BlockSpec((None, tm, K), lambda j, i: (slab, i, 0))
    else:
        x_spec = pl.BlockSpec((tm, K), lambda j, i: (i, 0))
    if w.ndim == 3:
        w_spec = pl.BlockSpec((None, K, tn), lambda j, i: (layer, 0, j))
    else:
        w_spec = pl.BlockSpec((K, tn), lambda j, i: (0, j))
    in_specs = [x_spec, w_spec]
    args = [x, w]
    for kind, arr in extras:
        if kind == "row":
            in_specs.append(pl.BlockSpec((1, tn), lambda j, i: (0, j)))
        elif kind == "full":
            in_specs.append(pl.BlockSpec((tm, tn), lambda j, i: (i, j)))
        elif kind == "rows":
            in_specs.append(pl.BlockSpec((tm, arr.shape[1]), lambda j, i: (i, 0)))
        elif kind == "const":
            in_specs.append(pl.BlockSpec(arr.shape, lambda j, i: (0,) * arr.ndim))
        else:
            raise ValueError(kind)
        args.append(arr)
    nj = N // tn
    out_shape = [jax.ShapeDtypeStruct((M, oc * nj), dt) for dt, oc in zip(out_dtypes, out_cols)]
    out_specs = [pl.BlockSpec((tm, oc), lambda j, i: (i, j)) for oc in out_cols]
    res = pl.pallas_call(
        functools.partial(_mm_kernel, n_extra=len(extras), n_out=len(out_dtypes), epilogue=epilogue),
        out_shape=out_shape,
        grid=(nj, M // tm),
        in_specs=in_specs,
        out_specs=out_specs,
        scratch_shapes=[pltpu.VMEM((K, tn), BF16)],
        compiler_params=_cparams("parallel", "arbitrary"),
        name=name,
    )(*args)
    return res[0] if len(res) == 1 else res


def _norm_kernel(h_ref, g_ref, o_ref):
    o_ref[...] = _rms(h_ref[...], g_ref[...]).astype(o_ref.dtype)


def _norm(h, g, out_dtype):
    M, D = h.shape
    return pl.pallas_call(
        _norm_kernel,
        out_shape=jax.ShapeDtypeStruct((M, D), out_dtype),
        grid=(M // ROW_TILE,),
        in_specs=[pl.BlockSpec((ROW_TILE, D), lambda i: (i, 0)),
                  pl.BlockSpec((1, D), lambda i: (0, 0))],
        out_specs=pl.BlockSpec((ROW_TILE, D), lambda i: (i, 0)),
        compiler_params=_cparams("parallel"),
        name="rmsnorm",
    )(h, g.reshape(1, D))


def _mix_kernel(h_ref, g_ref, mu_ref, sp_ref, xm_ref, xn_ref, carry_ref, *, tiles_per_seq, n_prompt_tiles):
    i = pl.program_id(0)
    xn = _rms(h_ref[...], g_ref[...])
    xn_ref[...] = xn
    prev = jnp.where(i % tiles_per_seq == 0, 0.0, carry_ref[...])
    row = lax.broadcasted_iota(I32, xn.shape, 0)
    shifted = jnp.where(row == 0, prev, pltpu.roll(xn, 1, 0))
    shifted = jnp.where(i >= n_prompt_tiles, sp_ref[...], shifted)
    carry_ref[...] = xn[ROW_TILE - 1:ROW_TILE, :]
    dx = shifted - xn
    for j in range(mu_ref.shape[0]):
        xm_ref[j] = (xn + dx * mu_ref[j:j + 1, :]).astype(BF16)


def _norm_shift_mix(h, g, mu, state_shift, seq, n_prompt):
    M, D = h.shape
    assert seq % ROW_TILE == 0 and n_prompt % ROW_TILE == 0 and (M - n_prompt) % ROW_TILE == 0
    npt = n_prompt // ROW_TILE
    nmix = mu.shape[0]
    return pl.pallas_call(
        functools.partial(_mix_kernel, tiles_per_seq=seq // ROW_TILE, n_prompt_tiles=npt),
        out_shape=[jax.ShapeDtypeStruct((nmix, M, D), BF16), jax.ShapeDtypeStruct((M, D), F32)],
        grid=(M // ROW_TILE,),
        in_specs=[pl.BlockSpec((ROW_TILE, D), lambda i: (i, 0)),
                  pl.BlockSpec((1, D), lambda i: (0, 0)),
                  pl.BlockSpec((nmix, D), lambda i: (0, 0)),
                  pl.BlockSpec((ROW_TILE, D), lambda i: (jnp.maximum(i - npt, 0), 0))],
        out_specs=[pl.BlockSpec((nmix, ROW_TILE, D), lambda i: (0, i, 0)),
                   pl.BlockSpec((ROW_TILE, D), lambda i: (i, 0))],
        scratch_shapes=[pltpu.VMEM((1, D), F32)],
        compiler_params=_cparams("arbitrary"),
        name="norm_shift_mix",
    )(h, g.reshape(1, D), mu, state_shift)


def _wkv_post_kernel(*refs):
    y_refs = refs[:HEADS_PER_TILE]
    r_ref, k_ref, v_ref, g_ref, lw_ref, lb_ref, rk_ref, o_ref = refs[HEADS_PER_TILE:]
    seg = _seg_ones()
    inv = 1.0 / RWKV_HEAD
    y = _heads_from_parts([y_ref[...] for y_ref in y_refs], r_ref.shape[1] // RWKV_HEAD)
    mean = _seg_sum(y, seg) * inv
    d = y - mean
    var = _seg_sum(d * d, seg) * inv
    yn = d * lax.rsqrt(var + GN_EPS) * lw_ref[...] + lb_ref[...]
    v = v_ref[...]
    bonus = _seg_sum(r_ref[...] * k_ref[...] * rk_ref[...], seg) * v
    o_ref[...] = ((yn + bonus) * g_ref[...]).astype(o_ref.dtype)


def _wkv_post(y_parts, r, k2, v, g, lnx_w, lnx_b, r_k):
    M, D = r.shape
    big = pl.BlockSpec((ROW_TILE, D), lambda i: (i, 0))
    part = pl.BlockSpec((ROW_TILE, D // HEADS_PER_TILE), lambda i: (i, 0))
    row = pl.BlockSpec((1, D), lambda i: (0, 0))
    return pl.pallas_call(
        _wkv_post_kernel,
        out_shape=jax.ShapeDtypeStruct((M, D), BF16),
        grid=(M // ROW_TILE,),
        in_specs=[part] * HEADS_PER_TILE + [big, big, big, big, row, row, row],
        out_specs=big,
        compiler_params=_cparams("parallel"),
        name="wkv_post",
    )(*y_parts, r, k2, v, g, lnx_w.reshape(1, D), lnx_b.reshape(1, D), r_k.reshape(1, D))


def _wkv_consts(ntile):
    seg = _seg_ones(STATE_LANES)
    rows = ntile * RWKV_HEAD
    rr = lax.broadcasted_iota(I32, (rows, STATE_LANES), 0) % RWKV_HEAD
    cc = lax.broadcasted_iota(I32, (rows, STATE_LANES), 1) % RWKV_HEAD
    diag = rr == cc
    eh = (lax.broadcasted_iota(I32, (SUBLANES, STATE_LANES), 0)
          == lax.broadcasted_iota(I32, (SUBLANES, STATE_LANES), 1) // RWKV_HEAD).astype(BF16)
    return seg, diag, eh


def _bcast_row(x, i, ntile):
    return jnp.concatenate(
        [jnp.broadcast_to(x[i:i + 1, j * STATE_LANES:(j + 1) * STATE_LANES], (RWKV_HEAD, STATE_LANES))
         for j in range(ntile)], axis=0)


def _wkv_step(S, vals, i, ntile, consts):
    r8, w8, k8, v8, a8, b8 = vals
    seg, diag, eh = consts
    sa = jnp.dot((S * _bcast_row(a8, i, ntile)).astype(BF16), seg, preferred_element_type=F32)
    vb = jnp.dot(jnp.where(diag, _bcast_row(v8, i, ntile), 0.0).astype(BF16), seg, preferred_element_type=F32)
    s_new = S * _bcast_row(w8, i, ntile) + sa * _bcast_row(b8, i, ntile) + vb * _bcast_row(k8, i, ntile)
    p = (s_new * _bcast_row(r8, i, ntile)).astype(BF16)
    y_t = lax.dot_general(eh, p, (((1,), (1,)), ((), ())), preferred_element_type=F32)
    return s_new, y_t


def _wkv_seq_kernel(*refs, nb, tc, ntile):
    in_refs = refs[:6 * nb]
    s0_ref = refs[6 * nb]
    y_refs = refs[6 * nb + 1:6 * nb + 1 + HEADS_PER_TILE]
    s_ref = refs[6 * nb + 1 + HEADS_PER_TILE]
    consts = _wkv_consts(ntile)

    @pl.when(pl.program_id(1) == 0)
    def _():
        s_ref[...] = s0_ref[...]

    def body(t8, carry):
        base = pl.multiple_of(t8 * SUBLANES, SUBLANES)
        vals = [[ref[pl.ds(base, SUBLANES), :] for ref in in_refs[6 * u:6 * u + 6]] for u in range(nb)]
        S = [s_ref[u] for u in range(nb)]
        ys = [[] for _ in range(nb)]
        for i in range(SUBLANES):
            for u in range(nb):
                S[u], y_t = _wkv_step(S[u], vals[u], i, ntile, consts)
                ys[u].append(y_t)
        for u in range(nb):
            s_ref[u] = S[u]
            for hs in range(HEADS_PER_TILE):
                y_refs[hs][u, pl.ds(base, SUBLANES), :] = jnp.concatenate(
                    [y[hs:hs + 1, :] for y in ys[u]], axis=0)
        return carry

    lax.fori_loop(0, tc // SUBLANES, body, 0)


def _wkv_scan_seq(rows, s0, *, nseq, seq, row0, nb, tc):
    D = rows[0].shape[1]
    ntile = D // STATE_LANES
    assert nseq % nb == 0 and seq % tc == 0 and row0 % tc == 0 and tc % SUBLANES == 0
    nt = seq // tc
    base = row0 // tc
    in_specs, args = [], []
    for u in range(nb):
        for arr in rows:
            in_specs.append(pl.BlockSpec((tc, D), lambda g, c, u=u: (base + (g * nb + u) * nt + c, 0)))
            args.append(arr)
    srows = ntile * RWKV_HEAD
    in_specs.append(pl.BlockSpec((nb, srows, STATE_LANES), lambda g, c: (g, 0, 0)))
    args.append(s0)
    half = D // HEADS_PER_TILE
    y_spec = pl.BlockSpec((nb, tc, half), lambda g, c: (g, c, 0))
    res = pl.pallas_call(
        functools.partial(_wkv_seq_kernel, nb=nb, tc=tc, ntile=ntile),
        out_shape=[jax.ShapeDtypeStruct((nseq, seq, half), F32)] * HEADS_PER_TILE
        + [jax.ShapeDtypeStruct((nseq, srows, STATE_LANES), F32)],
        grid=(nseq // nb, nt),
        in_specs=in_specs,
        out_specs=[y_spec] * HEADS_PER_TILE + [pl.BlockSpec((nb, srows, STATE_LANES), lambda g, c: (g, 0, 0))],
        compiler_params=_cparams("parallel", "arbitrary"),
        name="wkv_scan_seq",
    )(*args)
    return res[:HEADS_PER_TILE], res[HEADS_PER_TILE]


def _wkv_one_kernel(*refs, nu, ntile):
    in_refs = refs[:6]
    s0_ref = refs[6]
    y_refs = refs[7:7 + HEADS_PER_TILE]
    s_ref = refs[7 + HEADS_PER_TILE]
    consts = _wkv_consts(ntile)
    vals = [ref[...] for ref in in_refs]
    ys = []
    for u in range(nu):
        s_new, y_t = _wkv_step(s0_ref[u], vals, u, ntile, consts)
        s_ref[u] = s_new
        ys.append(y_t)
    for hs in range(HEADS_PER_TILE):
        y_refs[hs][...] = jnp.concatenate([y[hs:hs + 1, :] for y in ys], axis=0)


def _wkv_scan_one(rows, s0, *, nseq, row0, nu=SUBLANES):
    D = rows[0].shape[1]
    ntile = D // STATE_LANES
    assert nu == SUBLANES and nseq % nu == 0 and row0 % nu == 0
    base = row0 // nu
    srows = ntile * RWKV_HEAD
    half = D // HEADS_PER_TILE
    in_specs = [pl.BlockSpec((nu, D), lambda g: (base + g, 0)) for _ in rows]
    in_specs.append(pl.BlockSpec((nu, srows, STATE_LANES), lambda g: (g, 0, 0)))
    y_spec = pl.BlockSpec((nu, half), lambda g: (g, 0))
    res = pl.pallas_call(
        functools.partial(_wkv_one_kernel, nu=nu, ntile=ntile),
        out_shape=[jax.ShapeDtypeStruct((nseq, half), F32)] * HEADS_PER_TILE
        + [jax.ShapeDtypeStruct((nseq, srows, STATE_LANES), F32)],
        grid=(nseq // nu,),
        in_specs=in_specs,
        out_specs=[y_spec] * HEADS_PER_TILE + [pl.BlockSpec((nu, srows, STATE_LANES), lambda g: (g, 0, 0))],
        compiler_params=_cparams("parallel"),
        name="wkv_scan_one",
    )(*rows, s0)
    return res[:HEADS_PER_TILE], res[HEADS_PER_TILE]


def _state_to_tiles(s):
    n, H = s.shape[0], s.shape[1]
    s = s.reshape(n, H // HEADS_PER_TILE, HEADS_PER_TILE, RWKV_HEAD, RWKV_HEAD)
    return jnp.transpose(s, (0, 1, 3, 2, 4)).reshape(n, H // HEADS_PER_TILE * RWKV_HEAD, STATE_LANES)


def _tiles_to_state(s, H):
    n = s.shape[0]
    s = s.reshape(n, H // HEADS_PER_TILE, RWKV_HEAD, HEADS_PER_TILE, RWKV_HEAD)
    return jnp.transpose(s, (0, 1, 3, 2, 4)).reshape(n, H, RWKV_HEAD, RWKV_HEAD)


def _heads_from_parts(parts, H):
    return jnp.concatenate(
        [parts[h % HEADS_PER_TILE][:, (h // HEADS_PER_TILE) * RWKV_HEAD:(h // HEADS_PER_TILE + 1) * RWKV_HEAD]
         for h in range(H)], axis=1)


def _router_kernel(h_ref, g_ref, w_ref, b_ref, gid_ref, xb_ref, info_ref, wh_ref, wl_ref, *, n_groups):
    @pl.when(pl.program_id(0) == 0)
    def _():
        wh, wl = _split_bf16(w_ref[...])
        wh_ref[...] = wh
        wl_ref[...] = wl

    xn = _rms(h_ref[...], g_ref[...])
    xh, xl = _split_bf16(xn)
    xb_ref[...] = xn
    logits = (jnp.dot(xh, wh_ref[...], preferred_element_type=F32)
              + jnp.dot(xl, wh_ref[...], preferred_element_type=F32)
              + jnp.dot(xh, wl_ref[...], preferred_element_type=F32)) + b_ref[...]
    lane = lax.broadcasted_iota(I32, logits.shape, 1)
    lanef = lane.astype(F32)
    neg = -jnp.inf
    big = float(LANES)
    lg = jnp.where(lane < n_groups, logits, neg)
    mg = jnp.max(lg, -1, keepdims=True)
    g_top = jnp.min(jnp.where(lg == mg, lanef, big), -1, keepdims=True)
    p_sel = 1.0 / jnp.sum(jnp.exp(lg - mg), -1, keepdims=True)
    le = jnp.where(gid_ref[...] == g_top, logits, neg)
    v1 = jnp.max(le, -1, keepdims=True)
    i1 = jnp.min(jnp.where(le == v1, lanef, big), -1, keepdims=True)
    le2 = jnp.where(lanef == i1, neg, le)
    v2 = jnp.max(le2, -1, keepdims=True)
    i2 = jnp.min(jnp.where(le2 == v2, lanef, big), -1, keepdims=True)
    e2 = jnp.exp(v2 - v1)
    den = 1.0 + e2
    gate1 = (1.0 / den) * p_sel
    gate2 = (e2 / den) * p_sel
    info = jnp.where(lane == 0, i1 - n_groups,
                     jnp.where(lane == 1, i2 - n_groups,
                               jnp.where(lane == 2, gate1, jnp.where(lane == 3, gate2, 0.0))))
    info_ref[...] = info


def _router(h, g, w_route, b_route, n_groups, per_group):
    M, D = h.shape
    lane = jnp.arange(LANES, dtype=I32)
    is_expert = (lane >= n_groups) & (lane < n_groups * (1 + per_group))
    gid = jnp.where(is_expert, (lane - n_groups) // per_group, -1).astype(F32).reshape(1, LANES)
    return pl.pallas_call(
        functools.partial(_router_kernel, n_groups=n_groups),
        out_shape=[jax.ShapeDtypeStruct((M, D), F32), jax.ShapeDtypeStruct((M, LANES), F32)],
        grid=(M // ROW_TILE,),
        in_specs=[pl.BlockSpec((ROW_TILE, D), lambda i: (i, 0)),
                  pl.BlockSpec((1, D), lambda i: (0, 0)),
                  pl.BlockSpec((D, LANES), lambda i: (0, 0)),
                  pl.BlockSpec((1, LANES), lambda i: (0, 0)),
                  pl.BlockSpec((1, LANES), lambda i: (0, 0))],
        out_specs=[pl.BlockSpec((ROW_TILE, D), lambda i: (i, 0)),
                   pl.BlockSpec((ROW_TILE, LANES), lambda i: (i, 0))],
        scratch_shapes=[pltpu.VMEM((D, LANES), BF16), pltpu.VMEM((D, LANES), BF16)],
        compiler_params=_cparams("arbitrary"),
        name="moe_router",
    )(h, g.reshape(1, D), w_route, b_route, gid)


def _expert_kernel(te_ref, used_ref, x_ref, gate_ref, wg_ref, wu_ref, wd_ref, o_ref, wgb, wub, wdb):
    i = pl.program_id(0)
    new_expert = jnp.logical_or(i == 0, te_ref[i] != te_ref[jnp.maximum(i - 1, 0)])

    @pl.when(jnp.logical_and(i < used_ref[0], new_expert))
    def _():
        wgb[...] = wg_ref[0].astype(BF16)
        wub[...] = wu_ref[0].astype(BF16)
        wdb[...] = wd_ref[0].astype(BF16)

    @pl.when(i < used_ref[0])
    def _():
        x = x_ref[...].astype(BF16)
        hg = jnp.dot(x, wgb[...], preferred_element_type=F32)
        hu = jnp.dot(x, wub[...], preferred_element_type=F32)
        hid = (hg * jax.nn.sigmoid(hg)) * hu * gate_ref[...]
        o_ref[...] = jnp.dot(hid.astype(BF16), wdb[...], preferred_element_type=F32)

    @pl.when(i >= used_ref[0])
    def _():
        o_ref[...] = jnp.zeros_like(o_ref)


def _experts(tile_expert, used, xs, row_gate, w_gate, w_up, w_down, layer):
    R, D = xs.shape
    _, E, _, Fd = w_gate.shape
    nt = R // EXPERT_TILE
    return pl.pallas_call(
        _expert_kernel,
        out_shape=jax.ShapeDtypeStruct((R, D), F32),
        grid_spec=pltpu.PrefetchScalarGridSpec(
            num_scalar_prefetch=2,
            grid=(nt,),
            in_specs=[pl.BlockSpec((EXPERT_TILE, D), lambda i, te, u: (i, 0)),
                      pl.BlockSpec((EXPERT_TILE, 1), lambda i, te, u: (i, 0)),
                      pl.BlockSpec((None, 1, D, Fd), lambda i, te, u: (layer, te[i], 0, 0)),
                      pl.BlockSpec((None, 1, D, Fd), lambda i, te, u: (layer, te[i], 0, 0)),
                      pl.BlockSpec((None, 1, Fd, D), lambda i, te, u: (layer, te[i], 0, 0))],
            out_specs=pl.BlockSpec((EXPERT_TILE, D), lambda i, te, u: (i, 0)),
            scratch_shapes=[pltpu.VMEM((D, Fd), BF16), pltpu.VMEM((D, Fd), BF16), pltpu.VMEM((Fd, D), BF16)],
        ),
        compiler_params=_cparams("arbitrary"),
        name="moe_experts",
    )(tile_expert, used, xs, row_gate, w_gate, w_up, w_down)


def _moe(h, f_norm, w_group, b_group, w_expert, b_expert, w_gate, w_up, w_down, layer):
    M, D = h.shape
    G = w_group.shape[1]
    P = w_expert.shape[2]
    E = G * P
    assert G + E <= LANES
    w_route = jnp.zeros((D, LANES), F32)
    w_route = w_route.at[:, :G].set(w_group)
    w_route = w_route.at[:, G:G + E].set(jnp.transpose(w_expert, (1, 0, 2)).reshape(D, E))
    b_route = jnp.zeros((1, LANES), F32)
    b_route = b_route.at[0, :G].set(b_group)
    b_route = b_route.at[0, G:G + E].set(b_expert.reshape(E))
    xn, info = _router(h, f_norm, w_route, b_route, G, P)

    eid = info[:, :TOP_K].astype(I32).reshape(-1)
    gates = info[:, TOP_K:2 * TOP_K].reshape(-1)
    npair = M * TOP_K
    nt = -(-npair // EXPERT_TILE) + E
    R = nt * EXPERT_TILE
    onehot = (eid[:, None] == jnp.arange(E, dtype=I32)[None, :]).astype(I32)
    csum = jnp.cumsum(onehot, axis=0)
    counts = csum[-1]
    rank = jnp.sum(csum * onehot, axis=1) - 1
    padded = (counts + EXPERT_TILE - 1) // EXPERT_TILE * EXPERT_TILE
    pend = jnp.cumsum(padded)
    pstart = pend - padded
    dest = jnp.sum(onehot * pstart[None, :], axis=1) + rank
    token = (jnp.arange(npair, dtype=I32) // TOP_K).astype(F32)
    rows = jnp.zeros((R, 2), F32).at[dest].set(jnp.stack([token, gates], -1))
    row_token = rows[:, 0].astype(I32)
    tile_row0 = jnp.arange(nt, dtype=I32) * EXPERT_TILE
    tile_expert = jnp.minimum(jnp.sum((pend[None, :] <= tile_row0[:, None]).astype(I32), axis=1), E - 1)
    used = (pend[-1] // EXPERT_TILE).astype(I32).reshape(1)

    xs = jnp.take(xn, row_token, axis=0)
    ys = _experts(tile_expert, used, xs, rows[:, 1:2], w_gate, w_up, w_down, layer)
    dest = dest.reshape(M, TOP_K)
    y = jnp.take(ys, dest[:, 0], axis=0)
    for s in range(1, TOP_K):
        y = y + jnp.take(ys, dest[:, s], axis=0)
    return h + y


def _attn_prompt_kernel(qn_ref, qp_ref, kv_ref, wuk_ref, wuv_ref, o_ref, q_s, m_s, l_s, acc_s,
                        *, tq, tk, nh, dn, dc, scale):
    qi = pl.program_id(1)
    kv = pl.program_id(2)
    last = (qi * tq + tq - 1) // tk
    rows = nh * tq

    @pl.when(kv == 0)
    def _():
        for h in range(nh):
            qa = jnp.dot(qn_ref[:, h * dn:(h + 1) * dn], wuk_ref[h], preferred_element_type=F32) * scale
            q_s[h * tq:(h + 1) * tq, 0:dc] = qa.astype(BF16)
            q_s[h * tq:(h + 1) * tq, dc:] = qp_ref[:, h * LANES:(h + 1) * LANES]
        m_s[...] = jnp.full(m_s.shape, -jnp.inf, F32)
        l_s[...] = jnp.zeros(l_s.shape, F32)
        acc_s[...] = jnp.zeros(acc_s.shape, F32)

    def update(masked):
        kblk = kv_ref[...]
        s = lax.dot_general(q_s[...], kblk, (((1,), (1,)), ((), ())), preferred_element_type=F32)
        if masked:
            qpos = qi * tq + lax.broadcasted_iota(I32, (rows, tk), 0) % tq
            kpos = kv * tk + lax.broadcasted_iota(I32, (rows, tk), 1)
            s = jnp.where(kpos <= qpos, s, -jnp.inf)
        m_old = m_s[...]
        m_new = jnp.maximum(m_old, jnp.max(s, -1, keepdims=True))
        alpha = jnp.exp(m_old - m_new)
        p = jnp.exp(s - m_new)
        l_s[...] = alpha * l_s[...] + jnp.sum(p, -1, keepdims=True)
        acc_s[...] = alpha * acc_s[...] + jnp.dot(p.astype(BF16), kblk[:, 0:dc], preferred_element_type=F32)
        m_s[...] = m_new

    @pl.when(kv < last)
    def _():
        update(False)

    @pl.when(kv == last)
    def _():
        update(True)
        o = acc_s[...] / l_s[...]
        for h in range(nh):
            oh = jnp.dot(o[h * tq:(h + 1) * tq].astype(BF16), wuv_ref[h], preferred_element_type=F32)
            o_ref[:, h * oh.shape[1]:(h + 1) * oh.shape[1]] = oh.astype(o_ref.dtype)


def _attn_prompt(qn, qp, kvb, wuk_t, wuv_t, *, nb, seq, tq, tk, scale):
    nh, dn, dc = wuk_t.shape
    dv = wuv_t.shape[2]
    width = kvb.shape[1]
    assert width == dc + LANES and qp.shape[1] == nh * LANES
    nq, nk = seq // tq, seq // tk

    def kv_map(b, qi, kv):
        return (b * nk + jnp.minimum(kv, (qi * tq + tq - 1) // tk), 0)

    return pl.pallas_call(
        functools.partial(_attn_prompt_kernel, tq=tq, tk=tk, nh=nh, dn=dn, dc=dc, scale=scale),
        out_shape=jax.ShapeDtypeStruct((nb * seq, nh * dv), BF16),
        grid=(nb, nq, nk),
        in_specs=[pl.BlockSpec((tq, nh * dn), lambda b, qi, kv: (b * nq + qi, 0)),
                  pl.BlockSpec((tq, nh * LANES), lambda b, qi, kv: (b * nq + qi, 0)),
                  pl.BlockSpec((tk, width), kv_map),
                  pl.BlockSpec((nh, dn, dc), lambda b, qi, kv: (0, 0, 0)),
                  pl.BlockSpec((nh, dc, dv), lambda b, qi, kv: (0, 0, 0))],
        out_specs=pl.BlockSpec((tq, nh * dv), lambda b, qi, kv: (b * nq + qi, 0)),
        scratch_shapes=[pltpu.VMEM((nh * tq, width), BF16),
                        pltpu.VMEM((nh * tq, 1), F32),
                        pltpu.VMEM((nh * tq, 1), F32),
                        pltpu.VMEM((nh * tq, dc), F32)],
        compiler_params=_cparams("parallel", "parallel", "arbitrary"),
        name="mla_prompt_attn",
    )(qn, qp, kvb, wuk_t, wuv_t)


def _attn_decode_kernel(pt_ref, *refs, npg, dr, dc):
    qa_ref, qr_ref, kvn_ref = refs[:3]
    lat_refs = refs[3:3 + npg]
    kpe_refs = refs[3 + npg:3 + 2 * npg]
    o_ref = refs[3 + 2 * npg]
    m_s, l_s, acc_s, lat_s, kpe_s = refs[4 + 2 * npg:]
    g = pl.program_id(1)

    @pl.when(g == 0)
    def _():
        m_s[...] = jnp.full(m_s.shape, -jnp.inf, F32)
        l_s[...] = jnp.zeros(l_s.shape, F32)
        acc_s[...] = jnp.zeros(acc_s.shape, F32)

    qa = qa_ref[0]
    qr = qr_ref[0][:, 0:dr]
    nt = (((1,), (1,)), ((), ()))
    page = lat_refs[0].shape[1]
    for i in range(npg):
        lat_s[i * page:(i + 1) * page, :] = lat_refs[i][0].astype(BF16)
        kpe_s[:, i * page:(i + 1) * page] = kpe_refs[i][0].astype(BF16)
    lat = lat_s[...]
    s = (lax.dot_general(qa, lat, nt, preferred_element_type=F32)
         + jnp.dot(qr, kpe_s[...], preferred_element_type=F32))
    m_old = m_s[...]
    m_new = jnp.maximum(m_old, jnp.max(s, -1, keepdims=True))
    alpha = jnp.exp(m_old - m_new)
    p = jnp.exp(s - m_new)
    l_s[...] = alpha * l_s[...] + jnp.sum(p, -1, keepdims=True)
    acc_s[...] = alpha * acc_s[...] + jnp.dot(p.astype(BF16), lat, preferred_element_type=F32)
    m_s[...] = m_new

    @pl.when(g == pl.num_programs(1) - 1)
    def _():
        kvn = kvn_ref[0].astype(F32)
        s = (jnp.sum(qa.astype(F32) * kvn[:, 0:dc], -1, keepdims=True)
             + jnp.sum(qr.astype(F32) * kvn[:, dc:dc + dr], -1, keepdims=True))
        m_old = m_s[...]
        m_new = jnp.maximum(m_old, s)
        alpha = jnp.exp(m_old - m_new)
        p = jnp.exp(s - m_new)
        l = alpha * l_s[...] + p
        acc = alpha * acc_s[...] + p * kvn[:, 0:dc]
        o_ref[0] = (acc / l).astype(o_ref.dtype)


def _attn_decode(qa, qr, kvn, cache_latent, cache_kpe_t, page_table, *, dr, npg):
    nseq, npages = page_table.shape
    _, nh, dc = qa.shape
    width = kvn.shape[2]
    page = cache_latent.shape[1]
    assert npages % npg == 0

    def seq_spec(a, b):
        return pl.BlockSpec((1, a, b), lambda s, g, pt: (s, 0, 0))

    in_specs = [seq_spec(nh, dc), seq_spec(nh, LANES), seq_spec(1, width)]
    for i in range(npg):
        in_specs.append(pl.BlockSpec((1, page, dc), lambda s, g, pt, i=i: (pt[s * npages + g * npg + i], 0, 0)))
    for i in range(npg):
        in_specs.append(pl.BlockSpec((1, dr, page), lambda s, g, pt, i=i: (pt[s * npages + g * npg + i], 0, 0)))
    return pl.pallas_call(
        functools.partial(_attn_decode_kernel, npg=npg, dr=dr, dc=dc),
        out_shape=jax.ShapeDtypeStruct((nseq, nh, dc), BF16),
        grid_spec=pltpu.PrefetchScalarGridSpec(
            num_scalar_prefetch=1,
            grid=(nseq, npages // npg),
            in_specs=in_specs,
            out_specs=seq_spec(nh, dc),
            scratch_shapes=[pltpu.VMEM((nh, 1), F32), pltpu.VMEM((nh, 1), F32), pltpu.VMEM((nh, dc), F32),
                            pltpu.VMEM((npg * page, dc), BF16), pltpu.VMEM((dr, npg * page), BF16)],
        ),
        compiler_params=_cparams("parallel", "arbitrary"),
        name="mla_decode_attn",
    )(page_table.reshape(-1), qa, qr, kvn, *([cache_latent] * npg), *([cache_kpe_t] * npg))


def _head_mm_kernel(x_ref, w_ref, o_ref, *, scale):
    acc = jnp.dot(x_ref[...], w_ref[0], preferred_element_type=F32)
    if scale is not None:
        acc = acc * scale
    o_ref[...] = acc.astype(o_ref.dtype)


def _head_mm(x, w, scale=None):
    R = x.shape[0]
    nh, kin, kout = w.shape
    return pl.pallas_call(
        functools.partial(_head_mm_kernel, scale=scale),
        out_shape=jax.ShapeDtypeStruct((R, nh * kout), BF16),
        grid=(nh,),
        in_specs=[pl.BlockSpec((R, kin), lambda h: (0, h)),
                  pl.BlockSpec((1, kin, kout), lambda h: (h, 0, 0))],
        out_specs=pl.BlockSpec((R, kout), lambda h: (0, h)),
        compiler_params=_cparams("parallel"),
        name="head_mm",
    )(x, w)


def _rope_tables(pos, dr):
    half = dr // 2
    inv = ROPE_THETA ** (-2.0 * jnp.arange(half, dtype=F32) / dr)
    ang = pos.astype(F32)[:, None] * inv[None, :]
    cos, sin = jnp.cos(ang), jnp.sin(ang)
    rep = LANES // dr
    cos_t = jnp.tile(jnp.concatenate([cos, cos], -1), (1, rep))
    sin_t = jnp.tile(jnp.concatenate([-sin, sin], -1), (1, rep))
    return cos_t, sin_t


def _swap_halves(w, dr):
    return jnp.concatenate([w[..., dr // 2:], w[..., :dr // 2]], -1)


def _rwkv_layer(h, state_wkv, state_shift, n_prompt, nb, seq, aw, li):
    (a_norm, mu, w_r, w_k, w_v, w_o, w0, w1, w2, a0, a1, a2, g1, g2, k_k, k_a, r_k, lnx_w, lnx_b) = aw
    M, D = h.shape
    H = D // RWKV_HEAD
    nsample = M - n_prompt
    row = lambda x: ("row", x[li].reshape(1, D))
    xm, xn = _norm_shift_mix(h, a_norm[li], mu[li], state_shift, seq, n_prompt)
    jr, jw, jk, jv, ja, jg = range(6)

    r = _mm(xm, w_r, slab=jr, layer=li, name="rwkv_r")
    v = _mm(xm, w_v, slab=jv, layer=li, name="rwkv_v")
    th = _mm(xm, w1, slab=jw, layer=li, epilogue=jnp.tanh, out_dtypes=(BF16,), name="rwkv_w1")

    def decay_epilogue(z, w0r):
        u = -(w0r + z)
        softplus = jnp.maximum(u, 0.0) + jnp.log(1.0 + jnp.exp(-jnp.abs(u)))
        return jnp.exp(-jnp.exp(-softplus - 0.5))

    decay = _mm(th, w2, layer=li, extras=[row(w0)], epilogue=decay_epilogue, name="rwkv_w2")
    al = _mm(xm, a1, slab=ja, layer=li, out_dtypes=(BF16,), name="rwkv_a1")
    a_lr = _mm(al, a2, layer=li, extras=[row(a0)], name="rwkv_a2",
               epilogue=lambda z, a0r: jax.nn.sigmoid(a0r + z))
    gl = _mm(xm, g1, slab=jg, layer=li, epilogue=jax.nn.sigmoid, out_dtypes=(BF16,), name="rwkv_g1")
    g = _mm(gl, g2, layer=li, name="rwkv_g2")

    def k_epilogue(k, a, kkr, kar):
        seg = _seg_ones()
        kk = k * kkr
        nrm = jnp.sqrt(_seg_sum(kk * kk, seg))
        kk = kk / jnp.maximum(nrm, 1e-12)
        return k * (1.0 + (a - 1.0) * kar), -kk, kk * a

    k2, a_neg, b_pos = _mm(xm, w_k, slab=jk, layer=li, extras=[("full", a_lr), row(k_k), row(k_a)],
                           epilogue=k_epilogue, out_dtypes=(F32, F32, F32), name="rwkv_k")

    rows = (r, decay, k2, v, a_neg, b_pos)
    s0_p = jnp.zeros((nb, H // HEADS_PER_TILE * RWKV_HEAD, STATE_LANES), F32)
    yp_p, s_p = _wkv_scan_seq(rows, s0_p, nseq=nb, seq=seq, row0=0, nb=min(nb, 4), tc=min(seq, 32))
    yp_s, s_s = _wkv_scan_one(rows, _state_to_tiles(state_wkv), nseq=nsample, row0=n_prompt)
    y_parts = [jnp.concatenate([a.reshape(n_prompt, -1), b], 0) for a, b in zip(yp_p, yp_s)]

    yo = _wkv_post(y_parts, r, k2, v, g, lnx_w[li], lnx_b[li], r_k[li].reshape(D))
    h = _mm(yo, w_o, layer=li, extras=[("full", h)], epilogue=lambda acc, hh: hh + acc, name="rwkv_o")
    return h, _tiles_to_state(s_p, H), _tiles_to_state(s_s, H), xn


def _ple_layer(h, p_all, w_proj, g_norm, w_gate, li):
    hn = _norm(h, g_norm, BF16)
    pp = _mm(p_all, w_proj, slab=li, layer=li, name="ple_proj")
    return _mm(hn, w_gate, layer=li, extras=[("full", h), ("full", pp)], name="ple_gate",
               epilogue=lambda acc, hh, ppp: hh + ppp * jax.nn.sigmoid(acc))


def _shared_kv(h, pos, g_in, w_down, g_latent, dc, dr):
    M, D = h.shape
    cos_t, sin_t = _rope_tables(pos, dr)
    w_ext = jnp.concatenate([w_down, _swap_halves(w_down[:, dc:dc + dr], dr)], 1)
    hn = _norm(h, g_in, BF16)

    assert 2 * dr == LANES

    def epilogue(acc, gl, ct, st):
        lat = _rms(acc[:, :dc], gl)
        t = acc[:, dc:]
        kpe = t * ct + pltpu.roll(t, dr, 1) * st
        out = jnp.concatenate([lat, kpe], 1)
        return out, out

    return _mm(hn, w_ext, extras=[("const", g_latent.reshape(1, dc)), ("rows", cos_t), ("rows", sin_t)],
               epilogue=epilogue, out_dtypes=(F32, BF16), tn=dc + 2 * dr, name="kv_down")


def _mla_layer(h, pos, c_bf, n_prompt, nb, seq, cache_latent, cache_kpe_t, page_table, bw, li):
    b_norm, w_dq, g_q, w_uq, w_uk, w_uv, w_o = bw
    M, D = h.shape
    dc, nh, dn = w_uk.shape
    dv = w_uv.shape[2]
    ql = w_uq.shape[0]
    dr = w_uq.shape[1] // nh - dn
    scale = float(dn + dr) ** -0.5
    w_uq3 = w_uq.reshape(ql, nh, dn + dr)
    w_qn = w_uq3[:, :, :dn].reshape(ql, nh * dn)
    w_qr = w_uq3[:, :, dn:]
    assert 2 * dr == LANES
    w_qr_ext = jnp.concatenate([w_qr, _swap_halves(w_qr, dr)], -1).reshape(ql, nh * LANES)
    wuk_t = jnp.transpose(w_uk, (1, 2, 0)).astype(BF16)
    wuv_t = jnp.transpose(w_uv, (1, 0, 2)).astype(BF16)
    cos_t, sin_t = _rope_tables(pos, dr)

    xq = _norm(h, b_norm, BF16)
    cq = _mm(xq, w_dq, layer=li, extras=[("const", g_q.reshape(1, ql))], epilogue=_rms, out_dtypes=(BF16,),
             tn=ql, name="mla_dq")
    qn = _mm(cq, w_qn, out_dtypes=(BF16,), name="mla_uq_nope")

    def rope_epilogue(acc, ct, st):
        n = acc.shape[1]
        rot = acc * jnp.tile(ct, (1, nh)) + pltpu.roll(acc, n - dr, 1) * jnp.tile(st, (1, nh))
        keep = lax.broadcasted_iota(I32, acc.shape, 1) % LANES < dr
        return jnp.where(keep, rot * scale, 0.0)

    qp = _mm(cq, w_qr_ext, extras=[("rows", cos_t), ("rows", sin_t)], epilogue=rope_epilogue,
             out_dtypes=(BF16,), tn=nh * LANES, name="mla_uq_rope")

    o_p = _attn_prompt(qn, qp, c_bf, wuk_t, wuv_t, nb=nb, seq=seq, tq=min(seq, 128), tk=min(seq, 512),
                       scale=scale)
    ns = M - n_prompt
    qa_s = _head_mm(qn[n_prompt:], wuk_t, scale=scale).reshape(ns, nh, dc)
    ol_s = _attn_decode(qa_s, qp[n_prompt:].reshape(ns, nh, LANES), c_bf[n_prompt:].reshape(ns, 1, -1),
                        cache_latent, cache_kpe_t, page_table, dr=dr, npg=min(page_table.shape[1], 16))
    o_s = _head_mm(ol_s.reshape(ns, nh * dc), wuv_t)
    o = jnp.concatenate([o_p, o_s], 0)
    return _mm(o, w_o, layer=li, extras=[("full", h)], epilogue=lambda acc, hh: hh + acc, name="mla_o")


def kernel(x_prompt, x_sample, state_wkv, state_shift, cache_latent, cache_kpe, page_table, p_prompt, p_sample, a_norm, a_mu, a_wr, a_wk, a_wv, a_wo, a_w0, a_w1, a_w2, a_a0, a_a1, a_a2, a_g1, a_g2, a_kk, a_ka, a_rk, a_lnx_w, a_lnx_b, kv_norm, kv_wdown, kv_latent_norm, kv_wuk, kv_wuv, b_norm, b_wdq, b_qnorm, b_wuq, b_wo, f_norm, f_wgroup, f_bgroup, f_wexpert, f_bexpert, f_wgate, f_wup, f_wdown, pl_wproj, pl_norm, pl_wgate, final_norm):
    nb, seq, D = x_prompt.shape
    ns, dec_seq, _ = x_sample.shape
    assert dec_seq == 1
    depth = f_norm.shape[0]
    n_a = a_norm.shape[0]
    n_prompt = nb * seq
    dc = kv_latent_norm.shape[0]
    dr = kv_wdown.shape[1] - dc
    past_len = page_table.shape[1] * cache_latent.shape[1]
    pos = jnp.concatenate([jnp.tile(jnp.arange(seq, dtype=I32), nb), jnp.full((ns,), past_len, I32)])

    h = jnp.concatenate([x_prompt.reshape(n_prompt, D), x_sample.reshape(ns, D)], 0)
    p_all = jnp.concatenate([p_prompt.reshape(depth, n_prompt, -1), p_sample.reshape(depth, ns, -1)], 1)
    wkv_p, wkv_s, sh_p, sh_s = [], [], [], []
    c_f32 = c_bf = None
    cache_kpe_t = jnp.swapaxes(cache_kpe, 1, 2)
    aw = (a_norm, a_mu, a_wr, a_wk, a_wv, a_wo, a_w0, a_w1, a_w2, a_a0, a_a1, a_a2, a_g1, a_g2,
          a_kk, a_ka, a_rk, a_lnx_w, a_lnx_b)
    for i in range(depth):
        if i < n_a:
            h, s_p, s_s, xn = _rwkv_layer(h, state_wkv[i], state_shift[i], n_prompt, nb, seq, aw, i)
            wkv_p.append(s_p)
            wkv_s.append(s_s)
            sh_p.append(xn[seq - 1:n_prompt:seq])
            sh_s.append(xn[n_prompt:])
        else:
            j = i - n_a
            bw = (b_norm[j], b_wdq, b_qnorm[j], b_wuq[j], kv_wuk, kv_wuv, b_wo)
            h = _mla_layer(h, pos, c_bf, n_prompt, nb, seq, cache_latent, cache_kpe_t, page_table, bw, j)
        h = _moe(h, f_norm[i], f_wgroup[i], f_bgroup[i], f_wexpert[i], f_bexpert[i],
                 f_wgate, f_wup, f_wdown, i)
        h = _ple_layer(h, p_all, pl_wproj, pl_norm[i], pl_wgate, i)
        if i == n_a - 1:
            c_f32, c_bf = _shared_kv(h, pos, kv_norm, kv_wdown, kv_latent_norm, dc, dr)
    y = _norm(h, final_norm, F32)
    lat, kpe = c_f32[:, :dc], c_f32[:, dc:dc + dr]
    return (y[:n_prompt].reshape(nb, seq, D), y[n_prompt:].reshape(ns, 1, D),
            jnp.stack(wkv_p), jnp.stack(sh_p),
            lat[:n_prompt].reshape(nb, seq, dc), kpe[:n_prompt].reshape(nb, seq, dr),
            jnp.stack(wkv_s), jnp.stack(sh_s),
            lat[n_prompt:].reshape(ns, 1, dc), kpe[n_prompt:].reshape(ns, 1, dr))
```

```python
import functools

import jax
import jax.numpy as jnp
from jax import lax
from jax.experimental import pallas as pl
from jax.experimental.pallas import tpu as pltpu

F32 = jnp.float32
BF16 = jnp.bfloat16
I32 = jnp.int32

RMS_EPS = 1e-6
GN_EPS = 64e-5
ROPE_THETA = 10000.0
RWKV_HEAD = 64
LANES = 128
SUBLANES = 8
STATE_LANES = 256
HEADS_PER_TILE = STATE_LANES // RWKV_HEAD
ROW_TILE = 128
EXPERT_TILE = 128
TOP_K = 2
VMEM_LIMIT = 56 * 1024 * 1024


def _cparams(*sem):
    return pltpu.CompilerParams(dimension_semantics=sem, vmem_limit_bytes=VMEM_LIMIT)


def _pick_tile(n, target, mult=16):
    best = None
    for t in range(mult, min(n, target) + 1, mult):
        if n % t == 0:
            best = t
    assert best is not None, (n, target, mult)
    return best


def _rms(x, g):
    return x * lax.rsqrt(jnp.mean(x * x, -1, keepdims=True) + RMS_EPS) * g


def _split_bf16(x):
    hi = x.astype(BF16)
    lo = (x - hi.astype(F32)).astype(BF16)
    return hi, lo


def _seg_ones(width=LANES):
    r = lax.broadcasted_iota(I32, (width, width), 0) // RWKV_HEAD
    c = lax.broadcasted_iota(I32, (width, width), 1) // RWKV_HEAD
    return (r == c).astype(BF16)


def _seg_sum(x, seg):
    outs = []
    for j in range(x.shape[1] // LANES):
        hi, lo = _split_bf16(x[:, j * LANES:(j + 1) * LANES])
        outs.append(jnp.dot(hi, seg, preferred_element_type=F32)
                    + jnp.dot(lo, seg, preferred_element_type=F32))
    return jnp.concatenate(outs, axis=1)


def _mm_kernel(*refs, n_extra, n_out, epilogue):
    x_ref, w_ref = refs[0], refs[1]
    extra_refs = refs[2:2 + n_extra]
    out_refs = refs[2 + n_extra:2 + n_extra + n_out]
    wb_ref = refs[2 + n_extra + n_out]

    @pl.when(pl.program_id(1) == 0)
    def _():
        wb_ref[...] = w_ref[...].astype(BF16)

    acc = jnp.dot(x_ref[...].astype(BF16), wb_ref[...], preferred_element_type=F32)
    outs = epilogue(acc, *[r[...] for r in extra_refs]) if epilogue is not None else acc
    if not isinstance(outs, (tuple, list)):
        outs = (outs,)
    for o_ref, o in zip(out_refs, outs):
        o_ref[...] = o.astype(o_ref.dtype)


def _mm(x, w, *, slab=None, layer=None, epilogue=None, extras=(), out_dtypes=(F32,), out_cols=None,
        tn=512, tm_target=832, name="mm"):
    M, K = x.shape[-2:]
    N = w.shape[-1]
    tn = min(tn, N)
    assert N % tn == 0
    tm = _pick_tile(M, tm_target)
    out_cols = out_cols or [tn] * len(out_dtypes)
    if x.ndim == 3:
        x_spec = pl.BlockSpec((None, tm, K), lambda j, i: (slab, i, 0))
    else:
        x_spec = pl.BlockSpec((tm, K), lambda j, i: (i, 0))
    if w.ndim == 3:
        w_spec = pl.BlockSpec((None, K, tn), lambda j, i: (layer, 0, j))
    else:
        w_spec = pl.BlockSpec((K, tn), lambda j, i: (0, j))
    in_specs = [x_spec, w_spec]
    args = [x, w]
    for kind, arr in extras:
        if kind == "row":
            in_specs.append(pl.BlockSpec((1, tn), lambda j, i: (0, j)))
        elif kind == "full":
            in_specs.append(pl.BlockSpec((tm, tn), lambda j, i: (i, j)))
        elif kind == "rows":
            in_specs.append(pl.BlockSpec((tm, arr.shape[1]), lambda j, i: (i, 0)))
        elif kind == "const":
            in_specs.append(pl.BlockSpec(arr.shape, lambda j, i: (0,) * arr.ndim))
        else:
            raise ValueError(kind)
        args.append(arr)
    nj = N // tn
    out_shape = [jax.ShapeDtypeStruct((M, oc * nj), dt) for dt, oc in zip(out_dtypes, out_cols)]
    out_specs = [pl.BlockSpec((tm, oc), lambda j, i: (i, j)) for oc in out_cols]
    res = pl.pallas_call(
        functools.partial(_mm_kernel, n_extra=len(extras), n_out=len(out_dtypes), epilogue=epilogue),
        out_shape=out_shape,
        grid=(nj, M // tm),
        in_specs=in_specs,
        out_specs=out_specs,
        scratch_shapes=[pltpu.VMEM((K, tn), BF16)],
        compiler_params=_cparams("parallel", "arbitrary"),
        name=name,
    )(*args)
    return res[0] if len(res) == 1 else res


def _norm_kernel(h_ref, g_ref, o_ref):
    o_ref[...] = _rms(h_ref[...], g_ref[...]).astype(o_ref.dtype)


def _norm(h, g, out_dtype):
    M, D = h.shape
    return pl.pallas_call(
        _norm_kernel,
        out_shape=jax.ShapeDtypeStruct((M, D), out_dtype),
        grid=(M // ROW_TILE,),
        in_specs=[pl.BlockSpec((ROW_TILE, D), lambda i: (i, 0)),
                  pl.BlockSpec((1, D), lambda i: (0, 0))],
        out_specs=pl.BlockSpec((ROW_TILE, D), lambda i: (i, 0)),
        compiler_params=_cparams("parallel"),
        name="rmsnorm",
    )(h, g.reshape(1, D))


def _mix_kernel(h_ref, g_ref, mu_ref, sp_ref, xm_ref, xn_ref, carry_ref, *, tiles_per_seq, n_prompt_tiles):
    i = pl.program_id(0)
    xn = _rms(h_ref[...], g_ref[...])
    xn_ref[...] = xn
    prev = jnp.where(i % tiles_per_seq == 0, 0.0, carry_ref[...])
    row = lax.broadcasted_iota(I32, xn.shape, 0)
    shifted = jnp.where(row == 0, prev, pltpu.roll(xn, 1, 0))
    shifted = jnp.where(i >= n_prompt_tiles, sp_ref[...], shifted)
    carry_ref[...] = xn[ROW_TILE - 1:ROW_TILE, :]
    dx = shifted - xn
    for j in range(mu_ref.shape[0]):
        xm_ref[j] = (xn + dx * mu_ref[j:j + 1, :]).astype(BF16)


def _norm_shift_mix(h, g, mu, state_shift, seq, n_prompt):
    M, D = h.shape
    assert seq % ROW_TILE == 0 and n_prompt % ROW_TILE == 0 and (M - n_prompt) % ROW_TILE == 0
    npt = n_prompt // ROW_TILE
    nmix = mu.shape[0]
    return pl.pallas_call(
        functools.partial(_mix_kernel, tiles_per_seq=seq // ROW_TILE, n_prompt_tiles=npt),
        out_shape=[jax.ShapeDtypeStruct((nmix, M, D), BF16), jax.ShapeDtypeStruct((M, D), F32)],
        grid=(M // ROW_TILE,),
        in_specs=[pl.BlockSpec((ROW_TILE, D), lambda i: (i, 0)),
                  pl.BlockSpec((1, D), lambda i: (0, 0)),
                  pl.BlockSpec((nmix, D), lambda i: (0, 0)),
                  pl.BlockSpec((ROW_TILE, D), lambda i: (jnp.maximum(i - npt, 0), 0))],
        out_specs=[pl.BlockSpec((nmix, ROW_TILE, D), lambda i: (0, i, 0)),
                   pl.BlockSpec((ROW_TILE, D), lambda i: (i, 0))],
        scratch_shapes=[pltpu.VMEM((1, D), F32)],
        compiler_params=_cparams("arbitrary"),
        name="norm_shift_mix",
    )(h, g.reshape(1, D), mu, state_shift)


def _wkv_post_kernel(*refs):
    y_refs = refs[:HEADS_PER_TILE]
    r_ref, k_ref, v_ref, g_ref, lw_ref, lb_ref, rk_ref, o_ref = refs[HEADS_PER_TILE:]
    seg = _seg_ones()
    inv = 1.0 / RWKV_HEAD
    y = _heads_from_parts([y_ref[...] for y_ref in y_refs], r_ref.shape[1] // RWKV_HEAD)
    mean = _seg_sum(y, seg) * inv
    d = y - mean
    var = _seg_sum(d * d, seg) * inv
    yn = d * lax.rsqrt(var + GN_EPS) * lw_ref[...] + lb_ref[...]
    v = v_ref[...]
    bonus = _seg_sum(r_ref[...] * k_ref[...] * rk_ref[...], seg) * v
    o_ref[...] = ((yn + bonus) * g_ref[...]).astype(o_ref.dtype)


def _wkv_post(y_parts, r, k2, v, g, lnx_w, lnx_b, r_k):
    M, D = r.shape
    big = pl.BlockSpec((ROW_TILE, D), lambda i: (i, 0))
    part = pl.BlockSpec((ROW_TILE, D // HEADS_PER_TILE), lambda i: (i, 0))
    row = pl.BlockSpec((1, D), lambda i: (0, 0))
    return pl.pallas_call(
        _wkv_post_kernel,
        out_shape=jax.ShapeDtypeStruct((M, D), BF16),
        grid=(M // ROW_TILE,),
        in_specs=[part] * HEADS_PER_TILE + [big, big, big, big, row, row, row],
        out_specs=big,
        compiler_params=_cparams("parallel"),
        name="wkv_post",
    )(*y_parts, r, k2, v, g, lnx_w.reshape(1, D), lnx_b.reshape(1, D), r_k.reshape(1, D))


def _wkv_consts(ntile):
    seg = _seg_ones(STATE_LANES)
    rows = ntile * RWKV_HEAD
    rr = lax.broadcasted_iota(I32, (rows, STATE_LANES), 0) % RWKV_HEAD
    cc = lax.broadcasted_iota(I32, (rows, STATE_LANES), 1) % RWKV_HEAD
    diag = rr == cc
    eh = (lax.broadcasted_iota(I32, (SUBLANES, STATE_LANES), 0)
          == lax.broadcasted_iota(I32, (SUBLANES, STATE_LANES), 1) // RWKV_HEAD).astype(BF16)
    return seg, diag, eh


def _bcast_row(x, i, ntile):
    return jnp.concatenate(
        [jnp.broadcast_to(x[i:i + 1, j * STATE_LANES:(j + 1) * STATE_LANES], (RWKV_HEAD, STATE_LANES))
         for j in range(ntile)], axis=0)


def _wkv_step(S, vals, i, ntile, consts):
    r8, w8, k8, v8, a8, b8 = vals
    seg, diag, eh = consts
    sa = jnp.dot((S * _bcast_row(a8, i, ntile)).astype(BF16), seg, preferred_element_type=F32)
    vb = jnp.dot(jnp.where(diag, _bcast_row(v8, i, ntile), 0.0).astype(BF16), seg, preferred_element_type=F32)
    s_new = S * _bcast_row(w8, i, ntile) + sa * _bcast_row(b8, i, ntile) + vb * _bcast_row(k8, i, ntile)
    p = (s_new * _bcast_row(r8, i, ntile)).astype(BF16)
    y_t = lax.dot_general(eh, p, (((1,), (1,)), ((), ())), preferred_element_type=F32)
    return s_new, y_t


def _wkv_seq_kernel(*refs, nb, tc, ntile):
    in_refs = refs[:6 * nb]
    s0_ref = refs[6 * nb]
    y_refs = refs[6 * nb + 1:6 * nb + 1 + HEADS_PER_TILE]
    s_ref = refs[6 * nb + 1 + HEADS_PER_TILE]
    consts = _wkv_consts(ntile)

    @pl.when(pl.program_id(1) == 0)
    def _():
        s_ref[...] = s0_ref[...]

    def body(t8, carry):
        base = pl.multiple_of(t8 * SUBLANES, SUBLANES)
        vals = [[ref[pl.ds(base, SUBLANES), :] for ref in in_refs[6 * u:6 * u + 6]] for u in range(nb)]
        S = [s_ref[u] for u in range(nb)]
        ys = [[] for _ in range(nb)]
        for i in range(SUBLANES):
            for u in range(nb):
                S[u], y_t = _wkv_step(S[u], vals[u], i, ntile, consts)
                ys[u].append(y_t)
        for u in range(nb):
            s_ref[u] = S[u]
            for hs in range(HEADS_PER_TILE):
                y_refs[hs][u, pl.ds(base, SUBLANES), :] = jnp.concatenate(
                    [y[hs:hs + 1, :] for y in ys[u]], axis=0)
        return carry

    lax.fori_loop(0, tc // SUBLANES, body, 0)


def _wkv_scan_seq(rows, s0, *, nseq, seq, row0, nb, tc):
    D = rows[0].shape[1]
    ntile = D // STATE_LANES
    assert nseq % nb == 0 and seq % tc == 0 and row0 % tc == 0 and tc % SUBLANES == 0
    nt = seq // tc
    base = row0 // tc
    in_specs, args = [], []
    for u in range(nb):
        for arr in rows:
            in_specs.append(pl.BlockSpec((tc, D), lambda g, c, u=u: (base + (g * nb + u) * nt + c, 0)))
            args.append(arr)
    srows = ntile * RWKV_HEAD
    in_specs.append(pl.BlockSpec((nb, srows, STATE_LANES), lambda g, c: (g, 0, 0)))
    args.append(s0)
    half = D // HEADS_PER_TILE
    y_spec = pl.BlockSpec((nb, tc, half), lambda g, c: (g, c, 0))
    res = pl.pallas_call(
        functools.partial(_wkv_seq_kernel, nb=nb, tc=tc, ntile=ntile),
        out_shape=[jax.ShapeDtypeStruct((nseq, seq, half), F32)] * HEADS_PER_TILE
        + [jax.ShapeDtypeStruct((nseq, srows, STATE_LANES), F32)],
        grid=(nseq // nb, nt),
        in_specs=in_specs,
        out_specs=[y_spec] * HEADS_PER_TILE + [pl.BlockSpec((nb, srows, STATE_LANES), lambda g, c: (g, 0, 0))],
        compiler_params=_cparams("parallel", "arbitrary"),
        name="wkv_scan_seq",
    )(*args)
    return res[:HEADS_PER_TILE], res[HEADS_PER_TILE]


def _wkv_one_kernel(*refs, nu, ntile):
    in_refs = refs[:6]
    s0_ref = refs[6]
    y_refs = refs[7:7 + HEADS_PER_TILE]
    s_ref = refs[7 + HEADS_PER_TILE]
    consts = _wkv_consts(ntile)
    vals = [ref[...] for ref in in_refs]
    ys = []
    for u in range(nu):
        s_new, y_t = _wkv_step(s0_ref[u], vals, u, ntile, consts)
        s_ref[u] = s_new
        ys.append(y_t)
    for hs in range(HEADS_PER_TILE):
        y_refs[hs][...] = jnp.concatenate([y[hs:hs + 1, :] for y in ys], axis=0)


def _wkv_scan_one(rows, s0, *, nseq, row0, nu=SUBLANES):
    D = rows[0].shape[1]
    ntile = D // STATE_LANES
    assert nu == SUBLANES and nseq % nu == 0 and row0 % nu == 0
    base = row0 // nu
    srows = ntile * RWKV_HEAD
    half = D // HEADS_PER_TILE
    in_specs = [pl.BlockSpec((nu, D), lambda g: (base + g, 0)) for _ in rows]
    in_specs.append(pl.BlockSpec((nu, srows, STATE_LANES), lambda g: (g, 0, 0)))
    y_spec = pl.BlockSpec((nu, half), lambda g: (g, 0))
    res = pl.pallas_call(
        functools.partial(_wkv_one_kernel, nu=nu, ntile=ntile),
        out_shape=[jax.ShapeDtypeStruct((nseq, half), F32)] * HEADS_PER_TILE
        + [jax.ShapeDtypeStruct((nseq, srows, STATE_LANES), F32)],
        grid=(nseq // nu,),
        in_specs=in_specs,
        out_specs=[y_spec] * HEADS_PER_TILE + [pl.BlockSpec((nu, srows, STATE_LANES), lambda g: (g, 0, 0))],
        compiler_params=_cparams("parallel"),
        name="wkv_scan_one",
    )(*rows, s0)
    return res[:HEADS_PER_TILE], res[HEADS_PER_TILE]


def _state_to_tiles(s):
    n, H = s.shape[0], s.shape[1]
    s = s.reshape(n, H // HEADS_PER_TILE, HEADS_PER_TILE, RWKV_HEAD, RWKV_HEAD)
    return jnp.transpose(s, (0, 1, 3, 2, 4)).reshape(n, H // HEADS_PER_TILE * RWKV_HEAD, STATE_LANES)


def _tiles_to_state(s, H):
    n = s.shape[0]
    s = s.reshape(n, H // HEADS_PER_TILE, RWKV_HEAD, HEADS_PER_TILE, RWKV_HEAD)
    return jnp.transpose(s, (0, 1, 3, 2, 4)).reshape(n, H, RWKV_HEAD, RWKV_HEAD)


def _heads_from_parts(parts, H):
    return jnp.concatenate(
        [parts[h % HEADS_PER_TILE][:, (h // HEADS_PER_TILE) * RWKV_HEAD:(h // HEADS_PER_TILE + 1) * RWKV_HEAD]
         for h in range(H)], axis=1)


def _router_kernel(h_ref, g_ref, w_ref, b_ref, gid_ref, xb_ref, info_ref, wh_ref, wl_ref, *, n_groups):
    @pl.when(pl.program_id(0) == 0)
    def _():
        wh, wl = _split_bf16(w_ref[...])
        wh_ref[...] = wh
        wl_ref[...] = wl

    xn = _rms(h_ref[...], g_ref[...])
    xh, xl = _split_bf16(xn)
    xb_ref[...] = xn
    logits = (jnp.dot(xh, wh_ref[...], preferred_element_type=F32)
              + jnp.dot(xl, wh_ref[...], preferred_element_type=F32)
              + jnp.dot(xh, wl_ref[...], preferred_element_type=F32)) + b_ref[...]
    lane = lax.broadcasted_iota(I32, logits.shape, 1)
    lanef = lane.astype(F32)
    neg = -jnp.inf
    big = float(LANES)
    lg = jnp.where(lane < n_groups, logits, neg)
    mg = jnp.max(lg, -1, keepdims=True)
    g_top = jnp.min(jnp.where(lg == mg, lanef, big), -1, keepdims=True)
    p_sel = 1.0 / jnp.sum(jnp.exp(lg - mg), -1, keepdims=True)
    le = jnp.where(gid_ref[...] == g_top, logits, neg)
    v1 = jnp.max(le, -1, keepdims=True)
    i1 = jnp.min(jnp.where(le == v1, lanef, big), -1, keepdims=True)
    le2 = jnp.where(lanef == i1, neg, le)
    v2 = jnp.max(le2, -1, keepdims=True)
    i2 = jnp.min(jnp.where(le2 == v2, lanef, big), -1, keepdims=True)
    e2 = jnp.exp(v2 - v1)
    den = 1.0 + e2
    gate1 = (1.0 / den) * p_sel
    gate2 = (e2 / den) * p_sel
    info = jnp.where(lane == 0, i1 - n_groups,
                     jnp.where(lane == 1, i2 - n_groups,
                               jnp.where(lane == 2, gate1, jnp.where(lane == 3, gate2, 0.0))))
    info_ref[...] = info


def _router(h, g, w_route, b_route, n_groups, per_group):
    M, D = h.shape
    lane = jnp.arange(LANES, dtype=I32)
    is_expert = (lane >= n_groups) & (lane < n_groups * (1 + per_group))
    gid = jnp.where(is_expert, (lane - n_groups) // per_group, -1).astype(F32).reshape(1, LANES)
    return pl.pallas_call(
        functools.partial(_router_kernel, n_groups=n_groups),
        out_shape=[jax.ShapeDtypeStruct((M, D), F32), jax.ShapeDtypeStruct((M, LANES), F32)],
        grid=(M // ROW_TILE,),
        in_specs=[pl.BlockSpec((ROW_TILE, D), lambda i: (i, 0)),
                  pl.BlockSpec((1, D), lambda i: (0, 0)),
                  pl.BlockSpec((D, LANES), lambda i: (0, 0)),
                  pl.BlockSpec((1, LANES), lambda i: (0, 0)),
                  pl.BlockSpec((1, LANES), lambda i: (0, 0))],
        out_specs=[pl.BlockSpec((ROW_TILE, D), lambda i: (i, 0)),
                   pl.BlockSpec((ROW_TILE, LANES), lambda i: (i, 0))],
        scratch_shapes=[pltpu.VMEM((D, LANES), BF16), pltpu.VMEM((D, LANES), BF16)],
        compiler_params=_cparams("arbitrary"),
        name="moe_router",
    )(h, g.reshape(1, D), w_route, b_route, gid)


def _expert_kernel(te_ref, used_ref, tok_ref, x_hbm, gate_ref, wg_ref, wu_ref, wd_ref, o_ref,
                   xbuf, sem, wgb, wub, wdb):
    i = pl.program_id(0)
    used = used_ref[0]
    slot = i % 2
    new_expert = jnp.logical_or(i == 0, te_ref[i] != te_ref[jnp.maximum(i - 1, 0)])

    def row_copy(tile, r, sl):
        tok = tok_ref[tile * EXPERT_TILE + r]
        return pltpu.make_async_copy(x_hbm.at[pl.ds(tok, 1), :], xbuf.at[sl, pl.ds(r, 1), :], sem.at[sl])

    def start_tile(tile, sl):
        for r in range(EXPERT_TILE):
            row_copy(tile, r, sl).start()

    @pl.when(jnp.logical_and(i == 0, used > 0))
    def _():
        start_tile(0, 0)

    @pl.when(i + 1 < used)
    def _():
        start_tile(i + 1, 1 - slot)

    @pl.when(jnp.logical_and(i < used, new_expert))
    def _():
        wgb[...] = wg_ref[0].astype(BF16)
        wub[...] = wu_ref[0].astype(BF16)
        wdb[...] = wd_ref[0].astype(BF16)

    @pl.when(i < used)
    def _():
        for r in range(EXPERT_TILE):
            row_copy(i, r, slot).wait()
        x = xbuf[slot].astype(BF16)
        hg = jnp.dot(x, wgb[...], preferred_element_type=F32)
        hu = jnp.dot(x, wub[...], preferred_element_type=F32)
        hid = (hg * jax.nn.sigmoid(hg)) * hu * gate_ref[...]
        o_ref[...] = jnp.dot(hid.astype(BF16), wdb[...], preferred_element_type=F32)

    @pl.when(i >= used)
    def _():
        o_ref[...] = jnp.zeros_like(o_ref)


def _experts(tile_expert, used, row_token, x, row_gate, w_gate, w_up, w_down, layer):
    R = row_token.shape[0]
    D = x.shape[1]
    _, E, _, Fd = w_gate.shape
    nt = R // EXPERT_TILE
    return pl.pallas_call(
        _expert_kernel,
        out_shape=jax.ShapeDtypeStruct((R, D), F32),
        grid_spec=pltpu.PrefetchScalarGridSpec(
            num_scalar_prefetch=3,
            grid=(nt,),
            in_specs=[pl.BlockSpec(memory_space=pl.ANY),
                      pl.BlockSpec((EXPERT_TILE, 1), lambda i, te, u, tk: (i, 0)),
                      pl.BlockSpec((None, 1, D, Fd), lambda i, te, u, tk: (layer, te[i], 0, 0)),
                      pl.BlockSpec((None, 1, D, Fd), lambda i, te, u, tk: (layer, te[i], 0, 0)),
                      pl.BlockSpec((None, 1, Fd, D), lambda i, te, u, tk: (layer, te[i], 0, 0))],
            out_specs=pl.BlockSpec((EXPERT_TILE, D), lambda i, te, u, tk: (i, 0)),
            scratch_shapes=[pltpu.VMEM((2, EXPERT_TILE, D), F32), pltpu.SemaphoreType.DMA((2,)),
                            pltpu.VMEM((D, Fd), BF16), pltpu.VMEM((D, Fd), BF16), pltpu.VMEM((Fd, D), BF16)],
        ),
        compiler_params=_cparams("arbitrary"),
        name="moe_experts",
    )(tile_expert, used, row_token, x, row_gate, w_gate, w_up, w_down)


def _moe(h, f_norm, w_group, b_group, w_expert, b_expert, w_gate, w_up, w_down, layer):
    M, D = h.shape
    G = w_group.shape[1]
    P = w_expert.shape[2]
    E = G * P
    assert G + E <= LANES
    w_route = jnp.zeros((D, LANES), F32)
    w_route = w_route.at[:, :G].set(w_group)
    w_route = w_route.at[:, G:G + E].set(jnp.transpose(w_expert, (1, 0, 2)).reshape(D, E))
    b_route = jnp.zeros((1, LANES), F32)
    b_route = b_route.at[0, :G].set(b_group)
    b_route = b_route.at[0, G:G + E].set(b_expert.reshape(E))
    xn, info = _router(h, f_norm, w_route, b_route, G, P)

    eid = info[:, :TOP_K].astype(I32).reshape(-1)
    gates = info[:, TOP_K:2 * TOP_K].reshape(-1)
    npair = M * TOP_K
    nt = -(-npair // EXPERT_TILE) + E
    R = nt * EXPERT_TILE
    onehot = (eid[:, None] == jnp.arange(E, dtype=I32)[None, :]).astype(I32)
    csum = jnp.cumsum(onehot, axis=0)
    counts = csum[-1]
    rank = jnp.sum(csum * onehot, axis=1) - 1
    padded = (counts + EXPERT_TILE - 1) // EXPERT_TILE * EXPERT_TILE
    pend = jnp.cumsum(padded)
    pstart = pend - padded
    dest = jnp.sum(onehot * pstart[None, :], axis=1) + rank
    token = (jnp.arange(npair, dtype=I32) // TOP_K).astype(F32)
    rows = jnp.zeros((R, 2), F32).at[dest].set(jnp.stack([token, gates], -1))
    row_token = rows[:, 0].astype(I32)
    tile_row0 = jnp.arange(nt, dtype=I32) * EXPERT_TILE
    tile_expert = jnp.minimum(jnp.sum((pend[None, :] <= tile_row0[:, None]).astype(I32), axis=1), E - 1)
    used = (pend[-1] // EXPERT_TILE).astype(I32).reshape(1)

    ys = _experts(tile_expert, used, row_token, xn, rows[:, 1:2], w_gate, w_up, w_down, layer)
    dest = dest.reshape(M, TOP_K)
    y = jnp.take(ys, dest[:, 0], axis=0)
    for s in range(1, TOP_K):
        y = y + jnp.take(ys, dest[:, s], axis=0)
    return h + y


def _row_reduce(x, combine, reduce):
    t = x[:, 0:LANES]
    for j in range(1, x.shape[1] // LANES):
        t = combine(t, x[:, j * LANES:(j + 1) * LANES])
    return reduce(t, -1, keepdims=True)


def _attn_prompt_kernel(qn_ref, qp_ref, kv_ref, wuk_ref, wuv_ref, o_ref, q_s, m_s, l_s, acc_s,
                        *, tq, tk, nh, dn, dc, scale):
    qi = pl.program_id(1)
    kv = pl.program_id(2)
    last = (qi * tq + tq - 1) // tk
    rows = nh * tq

    @pl.when(kv == 0)
    def _():
        for h in range(nh):
            qa = jnp.dot(qn_ref[:, h * dn:(h + 1) * dn], wuk_ref[h], preferred_element_type=F32) * scale
            q_s[h * tq:(h + 1) * tq, 0:dc] = qa.astype(BF16)
            q_s[h * tq:(h + 1) * tq, dc:] = qp_ref[:, h * LANES:(h + 1) * LANES]
        m_s[...] = jnp.full(m_s.shape, -jnp.inf, F32)
        l_s[...] = jnp.zeros(l_s.shape, F32)
        acc_s[...] = jnp.zeros(acc_s.shape, F32)

    def update(masked):
        kblk = kv_ref[...]
        s = lax.dot_general(q_s[...], kblk, (((1,), (1,)), ((), ())), preferred_element_type=F32)
        if masked:
            qpos = qi * tq + lax.broadcasted_iota(I32, (rows, tk), 0) % tq
            kpos = kv * tk + lax.broadcasted_iota(I32, (rows, tk), 1)
            s = jnp.where(kpos <= qpos, s, -jnp.inf)
        m_old = m_s[...]
        m_new = jnp.maximum(m_old, _row_reduce(s, jnp.maximum, jnp.max))
        alpha = jnp.exp(m_old - m_new)
        p = jnp.exp(s - m_new)
        l_s[...] = alpha * l_s[...] + _row_reduce(p, jnp.add, jnp.sum)
        acc_s[...] = alpha * acc_s[...] + jnp.dot(p.astype(BF16), kblk[:, 0:dc], preferred_element_type=F32)
        m_s[...] = m_new

    @pl.when(kv < last)
    def _():
        update(False)

    @pl.when(kv == last)
    def _():
        update(True)
        o = acc_s[...] / l_s[...]
        for h in range(nh):
            oh = jnp.dot(o[h * tq:(h + 1) * tq].astype(BF16), wuv_ref[h], preferred_element_type=F32)
            o_ref[:, h * oh.shape[1]:(h + 1) * oh.shape[1]] = oh.astype(o_ref.dtype)


def _attn_prompt(qn, qp, kvb, wuk_t, wuv_t, *, nb, seq, tq, tk, scale):
    nh, dn, dc = wuk_t.shape
    dv = wuv_t.shape[2]
    width = kvb.shape[1]
    assert width == dc + LANES and qp.shape[1] == nh * LANES
    nq, nk = seq // tq, seq // tk

    def kv_map(b, qi, kv):
        return (b * nk + jnp.minimum(kv, (qi * tq + tq - 1) // tk), 0)

    return pl.pallas_call(
        functools.partial(_attn_prompt_kernel, tq=tq, tk=tk, nh=nh, dn=dn, dc=dc, scale=scale),
        out_shape=jax.ShapeDtypeStruct((nb * seq, nh * dv), BF16),
        grid=(nb, nq, nk),
        in_specs=[pl.BlockSpec((tq, nh * dn), lambda b, qi, kv: (b * nq + qi, 0)),
                  pl.BlockSpec((tq, nh * LANES), lambda b, qi, kv: (b * nq + qi, 0)),
                  pl.BlockSpec((tk, width), kv_map),
                  pl.BlockSpec((nh, dn, dc), lambda b, qi, kv: (0, 0, 0)),
                  pl.BlockSpec((nh, dc, dv), lambda b, qi, kv: (0, 0, 0))],
        out_specs=pl.BlockSpec((tq, nh * dv), lambda b, qi, kv: (b * nq + qi, 0)),
        scratch_shapes=[pltpu.VMEM((nh * tq, width), BF16),
                        pltpu.VMEM((nh * tq, 1), F32),
                        pltpu.VMEM((nh * tq, 1), F32),
                        pltpu.VMEM((nh * tq, dc), F32)],
        compiler_params=_cparams("parallel", "parallel", "arbitrary"),
        name="mla_prompt_attn",
    )(qn, qp, kvb, wuk_t, wuv_t)


def _attn_decode_kernel(pt_ref, *refs, npg, dr, dc):
    qa_ref, qr_ref, kvn_ref = refs[:3]
    lat_refs = refs[3:3 + npg]
    kpe_refs = refs[3 + npg:3 + 2 * npg]
    o_ref = refs[3 + 2 * npg]
    m_s, l_s, acc_s, lat_s, kpe_s = refs[4 + 2 * npg:]
    g = pl.program_id(1)

    @pl.when(g == 0)
    def _():
        m_s[...] = jnp.full(m_s.shape, -jnp.inf, F32)
        l_s[...] = jnp.zeros(l_s.shape, F32)
        acc_s[...] = jnp.zeros(acc_s.shape, F32)

    qa = qa_ref[0]
    qr = qr_ref[0][:, 0:dr]
    nt = (((1,), (1,)), ((), ()))
    page = lat_refs[0].shape[1]
    for i in range(npg):
        lat_s[i * page:(i + 1) * page, :] = lat_refs[i][0].astype(BF16)
        kpe_s[:, i * page:(i + 1) * page] = kpe_refs[i][0].astype(BF16)
    lat = lat_s[...]
    s = (lax.dot_general(qa, lat, nt, preferred_element_type=F32)
         + jnp.dot(qr, kpe_s[...], preferred_element_type=F32))
    m_old = m_s[...]
    m_new = jnp.maximum(m_old, jnp.max(s, -1, keepdims=True))
    alpha = jnp.exp(m_old - m_new)
    p = jnp.exp(s - m_new)
    l_s[...] = alpha * l_s[...] + jnp.sum(p, -1, keepdims=True)
    acc_s[...] = alpha * acc_s[...] + jnp.dot(p.astype(BF16), lat, preferred_element_type=F32)
    m_s[...] = m_new

    @pl.when(g == pl.num_programs(1) - 1)
    def _():
        kvn = kvn_ref[0].astype(F32)
        s = (jnp.sum(qa.astype(F32) * kvn[:, 0:dc], -1, keepdims=True)
             + jnp.sum(qr.astype(F32) * kvn[:, dc:dc + dr], -1, keepdims=True))
        m_old = m_s[...]
        m_new = jnp.maximum(m_old, s)
        alpha = jnp.exp(m_old - m_new)
        p = jnp.exp(s - m_new)
        l = alpha * l_s[...] + p
        acc = alpha * acc_s[...] + p * kvn[:, 0:dc]
        o_ref[0] = (acc / l).astype(o_ref.dtype)


def _attn_decode(qa, qr, kvn, cache_latent, cache_kpe_t, page_table, *, dr, npg):
    nseq, npages = page_table.shape
    _, nh, dc = qa.shape
    width = kvn.shape[2]
    page = cache_latent.shape[1]
    assert npages % npg == 0

    def seq_spec(a, b):
        return pl.BlockSpec((1, a, b), lambda s, g, pt: (s, 0, 0))

    in_specs = [seq_spec(nh, dc), seq_spec(nh, LANES), seq_spec(1, width)]
    for i in range(npg):
        in_specs.append(pl.BlockSpec((1, page, dc), lambda s, g, pt, i=i: (pt[s * npages + g * npg + i], 0, 0)))
    for i in range(npg):
        in_specs.append(pl.BlockSpec((1, dr, page), lambda s, g, pt, i=i: (pt[s * npages + g * npg + i], 0, 0)))
    return pl.pallas_call(
        functools.partial(_attn_decode_kernel, npg=npg, dr=dr, dc=dc),
        out_shape=jax.ShapeDtypeStruct((nseq, nh, dc), BF16),
        grid_spec=pltpu.PrefetchScalarGridSpec(
            num_scalar_prefetch=1,
            grid=(nseq, npages // npg),
            in_specs=in_specs,
            out_specs=seq_spec(nh, dc),
            scratch_shapes=[pltpu.VMEM((nh, 1), F32), pltpu.VMEM((nh, 1), F32), pltpu.VMEM((nh, dc), F32),
                            pltpu.VMEM((npg * page, dc), BF16), pltpu.VMEM((dr, npg * page), BF16)],
        ),
        compiler_params=_cparams("parallel", "arbitrary"),
        name="mla_decode_attn",
    )(page_table.reshape(-1), qa, qr, kvn, *([cache_latent] * npg), *([cache_kpe_t] * npg))


def _head_mm_kernel(x_ref, w_ref, o_ref, *, scale):
    acc = jnp.dot(x_ref[...], w_ref[0], preferred_element_type=F32)
    if scale is not None:
        acc = acc * scale
    o_ref[...] = acc.astype(o_ref.dtype)


def _head_mm(x, w, scale=None):
    R = x.shape[0]
    nh, kin, kout = w.shape
    return pl.pallas_call(
        functools.partial(_head_mm_kernel, scale=scale),
        out_shape=jax.ShapeDtypeStruct((R, nh * kout), BF16),
        grid=(nh,),
        in_specs=[pl.BlockSpec((R, kin), lambda h: (0, h)),
                  pl.BlockSpec((1, kin, kout), lambda h: (h, 0, 0))],
        out_specs=pl.BlockSpec((R, kout), lambda h: (0, h)),
        compiler_params=_cparams("parallel"),
        name="head_mm",
    )(x, w)


def _rope_tables(pos, dr):
    half = dr // 2
    inv = ROPE_THETA ** (-2.0 * jnp.arange(half, dtype=F32) / dr)
    ang = pos.astype(F32)[:, None] * inv[None, :]
    cos, sin = jnp.cos(ang), jnp.sin(ang)
    rep = LANES // dr
    cos_t = jnp.tile(jnp.concatenate([cos, cos], -1), (1, rep))
    sin_t = jnp.tile(jnp.concatenate([-sin, sin], -1), (1, rep))
    return cos_t, sin_t


def _swap_halves(w, dr):
    return jnp.concatenate([w[..., dr // 2:], w[..., :dr // 2]], -1)


def _rwkv_layer(h, state_wkv, state_shift, n_prompt, nb, seq, aw, li):
    (a_norm, mu, w_r, w_k, w_v, w_o, w0, w1, w2, a0, a1, a2, g1, g2, k_k, k_a, r_k, lnx_w, lnx_b) = aw
    M, D = h.shape
    H = D // RWKV_HEAD
    nsample = M - n_prompt
    row = lambda x: ("row", x[li].reshape(1, D))
    xm, xn = _norm_shift_mix(h, a_norm[li], mu[li], state_shift, seq, n_prompt)
    jr, jw, jk, jv, ja, jg = range(6)

    r = _mm(xm, w_r, slab=jr, layer=li, name="rwkv_r")
    v = _mm(xm, w_v, slab=jv, layer=li, name="rwkv_v")
    th = _mm(xm, w1, slab=jw, layer=li, epilogue=jnp.tanh, out_dtypes=(BF16,), name="rwkv_w1")

    def decay_epilogue(z, w0r):
        u = -(w0r + z)
        softplus = jnp.maximum(u, 0.0) + jnp.log(1.0 + jnp.exp(-jnp.abs(u)))
        return jnp.exp(-jnp.exp(-softplus - 0.5))

    decay = _mm(th, w2, layer=li, extras=[row(w0)], epilogue=decay_epilogue, name="rwkv_w2")
    al = _mm(xm, a1, slab=ja, layer=li, out_dtypes=(BF16,), name="rwkv_a1")
    a_lr = _mm(al, a2, layer=li, extras=[row(a0)], name="rwkv_a2",
               epilogue=lambda z, a0r: jax.nn.sigmoid(a0r + z))
    gl = _mm(xm, g1, slab=jg, layer=li, epilogue=jax.nn.sigmoid, out_dtypes=(BF16,), name="rwkv_g1")
    g = _mm(gl, g2, layer=li, name="rwkv_g2")

    def k_epilogue(k, a, kkr, kar):
        seg = _seg_ones()
        kk = k * kkr
        nrm = jnp.sqrt(_seg_sum(kk * kk, seg))
        kk = kk / jnp.maximum(nrm, 1e-12)
        return k * (1.0 + (a - 1.0) * kar), -kk, kk * a

    k2, a_neg, b_pos = _mm(xm, w_k, slab=jk, layer=li, extras=[("full", a_lr), row(k_k), row(k_a)],
                           epilogue=k_epilogue, out_dtypes=(F32, F32, F32), name="rwkv_k")

    rows = (r, decay, k2, v, a_neg, b_pos)
    s0_p = jnp.zeros((nb, H // HEADS_PER_TILE * RWKV_HEAD, STATE_LANES), F32)
    yp_p, s_p = _wkv_scan_seq(rows, s0_p, nseq=nb, seq=seq, row0=0, nb=min(nb, 4), tc=min(seq, 32))
    yp_s, s_s = _wkv_scan_one(rows, _state_to_tiles(state_wkv), nseq=nsample, row0=n_prompt)
    y_parts = [jnp.concatenate([a.reshape(n_prompt, -1), b], 0) for a, b in zip(yp_p, yp_s)]

    yo = _wkv_post(y_parts, r, k2, v, g, lnx_w[li], lnx_b[li], r_k[li].reshape(D))
    h = _mm(yo, w_o, layer=li, extras=[("full", h)], epilogue=lambda acc, hh: hh + acc, name="rwkv_o")
    return h, _tiles_to_state(s_p, H), _tiles_to_state(s_s, H), xn


def _ple_layer(h, p_all, w_proj, g_norm, w_gate, li):
    hn = _norm(h, g_norm, BF16)
    pp = _mm(p_all, w_proj, slab=li, layer=li, name="ple_proj")
    return _mm(hn, w_gate, layer=li, extras=[("full", h), ("full", pp)], name="ple_gate",
               epilogue=lambda acc, hh, ppp: hh + ppp * jax.nn.sigmoid(acc))


def _shared_kv(h, pos, g_in, w_down, g_latent, dc, dr):
    M, D = h.shape
    cos_t, sin_t = _rope_tables(pos, dr)
    w_ext = jnp.concatenate([w_down, _swap_halves(w_down[:, dc:dc + dr], dr)], 1)
    hn = _norm(h, g_in, BF16)

    assert 2 * dr == LANES

    def epilogue(acc, gl, ct, st):
        lat = _rms(acc[:, :dc], gl)
        t = acc[:, dc:]
        kpe = t * ct + pltpu.roll(t, dr, 1) * st
        out = jnp.concatenate([lat, kpe], 1)
        return out, out

    return _mm(hn, w_ext, extras=[("const", g_latent.reshape(1, dc)), ("rows", cos_t), ("rows", sin_t)],
               epilogue=epilogue, out_dtypes=(F32, BF16), tn=dc + 2 * dr, name="kv_down")


def _mla_layer(h, pos, c_bf, n_prompt, nb, seq, cache_latent, cache_kpe_t, page_table, bw, li):
    b_norm, w_dq, g_q, w_uq, w_uk, w_uv, w_o = bw
    M, D = h.shape
    dc, nh, dn = w_uk.shape
    dv = w_uv.shape[2]
    ql = w_uq.shape[0]
    dr = w_uq.shape[1] // nh - dn
    scale = float(dn + dr) ** -0.5
    w_uq3 = w_uq.reshape(ql, nh, dn + dr)
    w_qn = w_uq3[:, :, :dn].reshape(ql, nh * dn)
    w_qr = w_uq3[:, :, dn:]
    assert 2 * dr == LANES
    w_qr_ext = jnp.concatenate([w_qr, _swap_halves(w_qr, dr)], -1).reshape(ql, nh * LANES)
    wuk_t = jnp.transpose(w_uk, (1, 2, 0)).astype(BF16)
    wuv_t = jnp.transpose(w_uv, (1, 0, 2)).astype(BF16)
    cos_t, sin_t = _rope_tables(pos, dr)

    xq = _norm(h, b_norm, BF16)
    cq = _mm(xq, w_dq, layer=li, extras=[("const", g_q.reshape(1, ql))], epilogue=_rms, out_dtypes=(BF16,),
             tn=ql, name="mla_dq")
    qn = _mm(cq, w_qn, out_dtypes=(BF16,), name="mla_uq_nope")

    def rope_epilogue(acc, ct, st):
        n = acc.shape[1]
        rot = acc * jnp.tile(ct, (1, nh)) + pltpu.roll(acc, n - dr, 1) * jnp.tile(st, (1, nh))
        keep = lax.broadcasted_iota(I32, acc.shape, 1) % LANES < dr
        return jnp.where(keep, rot * scale, 0.0)

    qp = _mm(cq, w_qr_ext, extras=[("rows", cos_t), ("rows", sin_t)], epilogue=rope_epilogue,
             out_dtypes=(BF16,), tn=nh * LANES, name="mla_uq_rope")

    o_p = _attn_prompt(qn, qp, c_bf, wuk_t, wuv_t, nb=nb, seq=seq, tq=min(seq, 128), tk=min(seq, 512),
                       scale=scale)
    ns = M - n_prompt
    qa_s = _head_mm(qn[n_prompt:], wuk_t, scale=scale).reshape(ns, nh, dc)
    ol_s = _attn_decode(qa_s, qp[n_prompt:].reshape(ns, nh, LANES), c_bf[n_prompt:].reshape(ns, 1, -1),
                        cache_latent, cache_kpe_t, page_table, dr=dr, npg=min(page_table.shape[1], 32))
    o_s = _head_mm(ol_s.reshape(ns, nh * dc), wuv_t)
    o = jnp.concatenate([o_p, o_s], 0)
    return _mm(o, w_o, layer=li, extras=[("full", h)], epilogue=lambda acc, hh: hh + acc, name="mla_o")


def kernel(x_prompt, x_sample, state_wkv, state_shift, cache_latent, cache_kpe, page_table, p_prompt, p_sample, a_norm, a_mu, a_wr, a_wk, a_wv, a_wo, a_w0, a_w1, a_w2, a_a0, a_a1, a_a2, a_g1, a_g2, a_kk, a_ka, a_rk, a_lnx_w, a_lnx_b, kv_norm, kv_wdown, kv_latent_norm, kv_wuk, kv_wuv, b_norm, b_wdq, b_qnorm, b_wuq, b_wo, f_norm, f_wgroup, f_bgroup, f_wexpert, f_bexpert, f_wgate, f_wup, f_wdown, pl_wproj, pl_norm, pl_wgate, final_norm):
    nb, seq, D = x_prompt.shape
    ns, dec_seq, _ = x_sample.shape
    assert dec_seq == 1
    depth = f_norm.shape[0]
    n_a = a_norm.shape[0]
    n_prompt = nb * seq
    dc = kv_latent_norm.shape[0]
    dr = kv_wdown.shape[1] - dc
    past_len = page_table.shape[1] * cache_latent.shape[1]
    pos = jnp.concatenate([jnp.tile(jnp.arange(seq, dtype=I32), nb), jnp.full((ns,), past_len, I32)])

    h = jnp.concatenate([x_prompt.reshape(n_prompt, D), x_sample.reshape(ns, D)], 0)
    p_all = jnp.concatenate([p_prompt.reshape(depth, n_prompt, -1), p_sample.reshape(depth, ns, -1)], 1)
    wkv_p, wkv_s, sh_p, sh_s = [], [], [], []
    c_f32 = c_bf = None
    cache_kpe_t = jnp.swapaxes(cache_kpe, 1, 2)
    aw = (a_norm, a_mu, a_wr, a_wk, a_wv, a_wo, a_w0, a_w1, a_w2, a_a0, a_a1, a_a2, a_g1, a_g2,
          a_kk, a_ka, a_rk, a_lnx_w, a_lnx_b)
    for i in range(depth):
        if i < n_a:
            h, s_p, s_s, xn = _rwkv_layer(h, state_wkv[i], state_shift[i], n_prompt, nb, seq, aw, i)
            wkv_p.append(s_p)
            wkv_s.append(s_s)
            sh_p.append(xn[seq - 1:n_prompt:seq])
            sh_s.append(xn[n_prompt:])
        else:
            j = i - n_a
            bw = (b_norm[j], b_wdq, b_qnorm[j], b_wuq[j], kv_wuk, kv_wuv, b_wo)
            h = _mla_layer(h, pos, c_bf, n_prompt, nb, seq, cache_latent, cache_kpe_t, page_table, bw, j)
        h = _moe(h, f_norm[i], f_wgroup[i], f_bgroup[i], f_wexpert[i], f_bexpert[i],
                 f_wgate, f_wup, f_wdown, i)
        h = _ple_layer(h, p_all, pl_wproj, pl_norm[i], pl_wgate, i)
        if i == n_a - 1:
            c_f32, c_bf = _shared_kv(h, pos, kv_norm, kv_wdown, kv_latent_norm, dc, dr)
    y = _norm(h, final_norm, F32)
    lat, kpe = c_f32[:, :dc], c_f32[:, dc:dc + dr]
    return (y[:n_prompt].reshape(nb, seq, D), y[n_prompt:].reshape(ns, 1, D),
            jnp.stack(wkv_p), jnp.stack(sh_p),
            lat[:n_prompt].reshape(nb, seq, dc), kpe[:n_prompt].reshape(nb, seq, dr),
            jnp.stack(wkv_s), jnp.stack(sh_s),
            lat[n_prompt:].reshape(ns, 1, dc), kpe[n_prompt:].reshape(ns, 1, dr))
```

```python
import functools

import jax
import jax.numpy as jnp
from jax import lax
from jax.experimental import pallas as pl
from jax.experimental.pallas import tpu as pltpu

F32 = jnp.float32
BF16 = jnp.bfloat16
I32 = jnp.int32

RMS_EPS = 1e-6
GN_EPS = 64e-5
ROPE_THETA = 10000.0
RWKV_HEAD = 64
LANES = 128
SUBLANES = 8
STATE_LANES = 256
HEADS_PER_TILE = STATE_LANES // RWKV_HEAD
ROW_TILE = 128
EXPERT_TILE = 128
TOP_K = 2
VMEM_LIMIT = 56 * 1024 * 1024


def _cparams(*sem):
    return pltpu.CompilerParams(dimension_semantics=sem, vmem_limit_bytes=VMEM_LIMIT)


def _pick_tile(n, target, mult=16):
    best = None
    for t in range(mult, min(n, target) + 1, mult):
        if n % t == 0:
            best = t
    assert best is not None, (n, target, mult)
    return best


def _rms(x, g):
    return x * lax.rsqrt(jnp.mean(x * x, -1, keepdims=True) + RMS_EPS) * g


def _split_bf16(x):
    hi = x.astype(BF16)
    lo = (x - hi.astype(F32)).astype(BF16)
    return hi, lo


def _seg_ones(width=LANES):
    r = lax.broadcasted_iota(I32, (width, width), 0) // RWKV_HEAD
    c = lax.broadcasted_iota(I32, (width, width), 1) // RWKV_HEAD
    return (r == c).astype(BF16)


def _seg_sum(x, seg):
    outs = []
    for j in range(x.shape[1] // LANES):
        hi, lo = _split_bf16(x[:, j * LANES:(j + 1) * LANES])
        outs.append(jnp.dot(hi, seg, preferred_element_type=F32)
                    + jnp.dot(lo, seg, preferred_element_type=F32))
    return jnp.concatenate(outs, axis=1)


def _mm_kernel(*refs, n_extra, n_out, epilogue, normed):
    x_ref, w_ref = refs[0], refs[1]
    extra_refs = refs[2:2 + n_extra]
    out_refs = refs[2 + n_extra:2 + n_extra + n_out]
    wb_ref = refs[2 + n_extra + n_out]

    @pl.when(pl.program_id(1) == 0)
    def _():
        wb_ref[...] = w_ref[...].astype(BF16)

    x = x_ref[...]
    if normed:
        x = _rms(x, extra_refs[-1][...])
        extra_refs = extra_refs[:-1]
    acc = jnp.dot(x.astype(BF16), wb_ref[...], preferred_element_type=F32)
    outs = epilogue(acc, *[r[...] for r in extra_refs]) if epilogue is not None else acc
    if not isinstance(outs, (tuple, list)):
        outs = (outs,)
    for o_ref, o in zip(out_refs, outs):
        o_ref[...] = o.astype(o_ref.dtype)


def _mm(x, w, *, slab=None, layer=None, norm_gain=None, epilogue=None, extras=(), out_dtypes=(F32,),
        out_cols=None, tn=512, tm_target=832, name="mm"):
    M, K = x.shape[-2:]
    N = w.shape[-1]
    tn = min(tn, N)
    assert N % tn == 0
    tm = _pick_tile(M, tm_target)
    extras = list(extras)
    if norm_gain is not None:
        extras.append(("const", norm_gain.reshape(1, K)))
    out_cols = out_cols or [tn] * len(out_dtypes)
    if x.ndim == 3:
        x_spec = pl.BlockSpec((None, tm, K), lambda j, i: (slab, i, 0))
    else:
        x_spec = pl.BlockSpec((tm, K), lambda j, i: (i, 0))
    if w.ndim == 3:
        w_spec = pl.BlockSpec((None, K, tn), lambda j, i: (layer, 0, j))
    else:
        w_spec = pl.BlockSpec((K, tn), lambda j, i: (0, j))
    in_specs = [x_spec, w_spec]
    args = [x, w]
    for kind, arr in extras:
        if kind == "row":
            in_specs.append(pl.BlockSpec((1, tn), lambda j, i: (0, j)))
        elif kind == "full":
            in_specs.append(pl.BlockSpec((tm, tn), lambda j, i: (i, j)))
        elif kind == "rows":
            in_specs.append(pl.BlockSpec((tm, arr.shape[1]), lambda j, i: (i, 0)))
        elif kind == "const":
            in_specs.append(pl.BlockSpec(arr.shape, lambda j, i: (0,) * arr.ndim))
        else:
            raise ValueError(kind)
        args.append(arr)
    nj = N // tn
    out_shape = [jax.ShapeDtypeStruct((M, oc * nj), dt) for dt, oc in zip(out_dtypes, out_cols)]
    out_specs = [pl.BlockSpec((tm, oc), lambda j, i: (i, j)) for oc in out_cols]
    res = pl.pallas_call(
        functools.partial(_mm_kernel, n_extra=len(extras), n_out=len(out_dtypes), epilogue=epilogue,
                          normed=norm_gain is not None),
        out_shape=out_shape,
        grid=(nj, M // tm),
        in_specs=in_specs,
        out_specs=out_specs,
        scratch_shapes=[pltpu.VMEM((K, tn), BF16)],
        compiler_params=_cparams("parallel", "arbitrary"),
        name=name,
    )(*args)
    return res[0] if len(res) == 1 else res


def _norm_kernel(h_ref, g_ref, o_ref):
    o_ref[...] = _rms(h_ref[...], g_ref[...]).astype(o_ref.dtype)


def _norm(h, g, out_dtype):
    M, D = h.shape
    return pl.pallas_call(
        _norm_kernel,
        out_shape=jax.ShapeDtypeStruct((M, D), out_dtype),
        grid=(M // ROW_TILE,),
        in_specs=[pl.BlockSpec((ROW_TILE, D), lambda i: (i, 0)),
                  pl.BlockSpec((1, D), lambda i: (0, 0))],
        out_specs=pl.BlockSpec((ROW_TILE, D), lambda i: (i, 0)),
        compiler_params=_cparams("parallel"),
        name="rmsnorm",
    )(h, g.reshape(1, D))


def _mix_kernel(h_ref, g_ref, mu_ref, sp_ref, xm_ref, xn_ref, carry_ref, *, tiles_per_seq, n_prompt_tiles):
    i = pl.program_id(0)
    xn = _rms(h_ref[...], g_ref[...])
    xn_ref[...] = xn
    prev = jnp.where(i % tiles_per_seq == 0, 0.0, carry_ref[...])
    row = lax.broadcasted_iota(I32, xn.shape, 0)
    shifted = jnp.where(row == 0, prev, pltpu.roll(xn, 1, 0))
    shifted = jnp.where(i >= n_prompt_tiles, sp_ref[...], shifted)
    carry_ref[...] = xn[ROW_TILE - 1:ROW_TILE, :]
    dx = shifted - xn
    for j in range(mu_ref.shape[0]):
        xm_ref[j] = (xn + dx * mu_ref[j:j + 1, :]).astype(BF16)


def _norm_shift_mix(h, g, mu, state_shift, seq, n_prompt):
    M, D = h.shape
    assert seq % ROW_TILE == 0 and n_prompt % ROW_TILE == 0 and (M - n_prompt) % ROW_TILE == 0
    npt = n_prompt // ROW_TILE
    nmix = mu.shape[0]
    return pl.pallas_call(
        functools.partial(_mix_kernel, tiles_per_seq=seq // ROW_TILE, n_prompt_tiles=npt),
        out_shape=[jax.ShapeDtypeStruct((nmix, M, D), BF16), jax.ShapeDtypeStruct((M, D), F32)],
        grid=(M // ROW_TILE,),
        in_specs=[pl.BlockSpec((ROW_TILE, D), lambda i: (i, 0)),
                  pl.BlockSpec((1, D), lambda i: (0, 0)),
                  pl.BlockSpec((nmix, D), lambda i: (0, 0)),
                  pl.BlockSpec((ROW_TILE, D), lambda i: (jnp.maximum(i - npt, 0), 0))],
        out_specs=[pl.BlockSpec((nmix, ROW_TILE, D), lambda i: (0, i, 0)),
                   pl.BlockSpec((ROW_TILE, D), lambda i: (i, 0))],
        scratch_shapes=[pltpu.VMEM((1, D), F32)],
        compiler_params=_cparams("arbitrary"),
        name="norm_shift_mix",
    )(h, g.reshape(1, D), mu, state_shift)


def _wkv_post_kernel(*refs):
    y_refs = refs[:HEADS_PER_TILE]
    r_ref, k_ref, v_ref, g_ref, lw_ref, lb_ref, rk_ref, o_ref = refs[HEADS_PER_TILE:]
    seg = _seg_ones()
    inv = 1.0 / RWKV_HEAD
    y = _heads_from_parts([y_ref[...] for y_ref in y_refs], r_ref.shape[1] // RWKV_HEAD)
    mean = _seg_sum(y, seg) * inv
    d = y - mean
    var = _seg_sum(d * d, seg) * inv
    yn = d * lax.rsqrt(var + GN_EPS) * lw_ref[...] + lb_ref[...]
    v = v_ref[...]
    bonus = _seg_sum(r_ref[...] * k_ref[...] * rk_ref[...], seg) * v
    o_ref[...] = ((yn + bonus) * g_ref[...]).astype(o_ref.dtype)


def _wkv_post(y_parts, r, k2, v, g, lnx_w, lnx_b, r_k):
    M, D = r.shape
    big = pl.BlockSpec((ROW_TILE, D), lambda i: (i, 0))
    part = pl.BlockSpec((ROW_TILE, D // HEADS_PER_TILE), lambda i: (i, 0))
    row = pl.BlockSpec((1, D), lambda i: (0, 0))
    return pl.pallas_call(
        _wkv_post_kernel,
        out_shape=jax.ShapeDtypeStruct((M, D), BF16),
        grid=(M // ROW_TILE,),
        in_specs=[part] * HEADS_PER_TILE + [big, big, big, big, row, row, row],
        out_specs=big,
        compiler_params=_cparams("parallel"),
        name="wkv_post",
    )(*y_parts, r, k2, v, g, lnx_w.reshape(1, D), lnx_b.reshape(1, D), r_k.reshape(1, D))


def _wkv_consts(ntile):
    seg = _seg_ones(STATE_LANES)
    rows = ntile * RWKV_HEAD
    rr = lax.broadcasted_iota(I32, (rows, STATE_LANES), 0) % RWKV_HEAD
    cc = lax.broadcasted_iota(I32, (rows, STATE_LANES), 1) % RWKV_HEAD
    diag = rr == cc
    eh = (lax.broadcasted_iota(I32, (SUBLANES, STATE_LANES), 0)
          == lax.broadcasted_iota(I32, (SUBLANES, STATE_LANES), 1) // RWKV_HEAD).astype(BF16)
    return seg, diag, eh


def _bcast_row(x, i, ntile):
    return jnp.concatenate(
        [jnp.broadcast_to(x[i:i + 1, j * STATE_LANES:(j + 1) * STATE_LANES], (RWKV_HEAD, STATE_LANES))
         for j in range(ntile)], axis=0)


def _wkv_step(S, vals, i, ntile, consts):
    r8, w8, k8, v8, a8, b8 = vals
    seg, diag, eh = consts
    sa = jnp.dot((S * _bcast_row(a8, i, ntile)).astype(BF16), seg, preferred_element_type=F32)
    vb = jnp.dot(jnp.where(diag, _bcast_row(v8, i, ntile), 0.0).astype(BF16), seg, preferred_element_type=F32)
    s_new = S * _bcast_row(w8, i, ntile) + sa * _bcast_row(b8, i, ntile) + vb * _bcast_row(k8, i, ntile)
    p = (s_new * _bcast_row(r8, i, ntile)).astype(BF16)
    y_t = lax.dot_general(eh, p, (((1,), (1,)), ((), ())), preferred_element_type=F32)
    return s_new, y_t


def _wkv_seq_kernel(*refs, nb, tc, ntile):
    in_refs = refs[:6 * nb]
    s0_ref = refs[6 * nb]
    y_refs = refs[6 * nb + 1:6 * nb + 1 + HEADS_PER_TILE]
    s_ref = refs[6 * nb + 1 + HEADS_PER_TILE]
    consts = _wkv_consts(ntile)

    @pl.when(pl.program_id(1) == 0)
    def _():
        s_ref[...] = s0_ref[...]

    def body(t8, carry):
        base = pl.multiple_of(t8 * SUBLANES, SUBLANES)
        vals = [[ref[pl.ds(base, SUBLANES), :] for ref in in_refs[6 * u:6 * u + 6]] for u in range(nb)]
        S = [s_ref[u] for u in range(nb)]
        ys = [[] for _ in range(nb)]
        for i in range(SUBLANES):
            for u in range(nb):
                S[u], y_t = _wkv_step(S[u], vals[u], i, ntile, consts)
                ys[u].append(y_t)
        for u in range(nb):
            s_ref[u] = S[u]
            for hs in range(HEADS_PER_TILE):
                y_refs[hs][u, pl.ds(base, SUBLANES), :] = jnp.concatenate(
                    [y[hs:hs + 1, :] for y in ys[u]], axis=0)
        return carry

    lax.fori_loop(0, tc // SUBLANES, body, 0)


def _wkv_scan_seq(rows, s0, *, nseq, seq, row0, nb, tc):
    D = rows[0].shape[1]
    ntile = D // STATE_LANES
    assert nseq % nb == 0 and seq % tc == 0 and row0 % tc == 0 and tc % SUBLANES == 0
    nt = seq // tc
    base = row0 // tc
    in_specs, args = [], []
    for u in range(nb):
        for arr in rows:
            in_specs.append(pl.BlockSpec((tc, D), lambda g, c, u=u: (base + (g * nb + u) * nt + c, 0)))
            args.append(arr)
    srows = ntile * RWKV_HEAD
    in_specs.append(pl.BlockSpec((nb, srows, STATE_LANES), lambda g, c: (g, 0, 0)))
    args.append(s0)
    half = D // HEADS_PER_TILE
    y_spec = pl.BlockSpec((nb, tc, half), lambda g, c: (g, c, 0))
    res = pl.pallas_call(
        functools.partial(_wkv_seq_kernel, nb=nb, tc=tc, ntile=ntile),
        out_shape=[jax.ShapeDtypeStruct((nseq, seq, half), F32)] * HEADS_PER_TILE
        + [jax.ShapeDtypeStruct((nseq, srows, STATE_LANES), F32)],
        grid=(nseq // nb, nt),
        in_specs=in_specs,
        out_specs=[y_spec] * HEADS_PER_TILE + [pl.BlockSpec((nb, srows, STATE_LANES), lambda g, c: (g, 0, 0))],
        compiler_params=_cparams("parallel", "arbitrary"),
        name="wkv_scan_seq",
    )(*args)
    return res[:HEADS_PER_TILE], res[HEADS_PER_TILE]


def _wkv_one_kernel(*refs, nu, ntile):
    in_refs = refs[:6]
    s0_ref = refs[6]
    y_refs = refs[7:7 + HEADS_PER_TILE]
    s_ref = refs[7 + HEADS_PER_TILE]
    consts = _wkv_consts(ntile)
    vals = [ref[...] for ref in in_refs]
    ys = []
    for u in range(nu):
        s_new, y_t = _wkv_step(s0_ref[u], vals, u, ntile, consts)
        s_ref[u] = s_new
        ys.append(y_t)
    for hs in range(HEADS_PER_TILE):
        y_refs[hs][...] = jnp.concatenate([y[hs:hs + 1, :] for y in ys], axis=0)


def _wkv_scan_one(rows, s0, *, nseq, row0, nu=SUBLANES):
    D = rows[0].shape[1]
    ntile = D // STATE_LANES
    assert nu == SUBLANES and nseq % nu == 0 and row0 % nu == 0
    base = row0 // nu
    srows = ntile * RWKV_HEAD
    half = D // HEADS_PER_TILE
    in_specs = [pl.BlockSpec((nu, D), lambda g: (base + g, 0)) for _ in rows]
    in_specs.append(pl.BlockSpec((nu, srows, STATE_LANES), lambda g: (g, 0, 0)))
    y_spec = pl.BlockSpec((nu, half), lambda g: (g, 0))
    res = pl.pallas_call(
        functools.partial(_wkv_one_kernel, nu=nu, ntile=ntile),
        out_shape=[jax.ShapeDtypeStruct((nseq, half), F32)] * HEADS_PER_TILE
        + [jax.ShapeDtypeStruct((nseq, srows, STATE_LANES), F32)],
        grid=(nseq // nu,),
        in_specs=in_specs,
        out_specs=[y_spec] * HEADS_PER_TILE + [pl.BlockSpec((nu, srows, STATE_LANES), lambda g: (g, 0, 0))],
        compiler_params=_cparams("parallel"),
        name="wkv_scan_one",
    )(*rows, s0)
    return res[:HEADS_PER_TILE], res[HEADS_PER_TILE]


def _state_to_tiles(s):
    n, H = s.shape[0], s.shape[1]
    s = s.reshape(n, H // HEADS_PER_TILE, HEADS_PER_TILE, RWKV_HEAD, RWKV_HEAD)
    return jnp.transpose(s, (0, 1, 3, 2, 4)).reshape(n, H // HEADS_PER_TILE * RWKV_HEAD, STATE_LANES)


def _tiles_to_state(s, H):
    n = s.shape[0]
    s = s.reshape(n, H // HEADS_PER_TILE, RWKV_HEAD, HEADS_PER_TILE, RWKV_HEAD)
    return jnp.transpose(s, (0, 1, 3, 2, 4)).reshape(n, H, RWKV_HEAD, RWKV_HEAD)


def _heads_from_parts(parts, H):
    return jnp.concatenate(
        [parts[h % HEADS_PER_TILE][:, (h // HEADS_PER_TILE) * RWKV_HEAD:(h // HEADS_PER_TILE + 1) * RWKV_HEAD]
         for h in range(H)], axis=1)


def _router_kernel(h_ref, g_ref, w_ref, b_ref, gid_ref, xb_ref, info_ref, wh_ref, wl_ref, *, n_groups):
    @pl.when(pl.program_id(0) == 0)
    def _():
        wh, wl = _split_bf16(w_ref[...])
        wh_ref[...] = wh
        wl_ref[...] = wl

    xn = _rms(h_ref[...], g_ref[...])
    xh, xl = _split_bf16(xn)
    xb_ref[...] = xn
    logits = (jnp.dot(xh, wh_ref[...], preferred_element_type=F32)
              + jnp.dot(xl, wh_ref[...], preferred_element_type=F32)
              + jnp.dot(xh, wl_ref[...], preferred_element_type=F32)) + b_ref[...]
    lane = lax.broadcasted_iota(I32, logits.shape, 1)
    lanef = lane.astype(F32)
    neg = -jnp.inf
    big = float(LANES)
    lg = jnp.where(lane < n_groups, logits, neg)
    mg = jnp.max(lg, -1, keepdims=True)
    g_top = jnp.min(jnp.where(lg == mg, lanef, big), -1, keepdims=True)
    p_sel = 1.0 / jnp.sum(jnp.exp(lg - mg), -1, keepdims=True)
    le = jnp.where(gid_ref[...] == g_top, logits, neg)
    v1 = jnp.max(le, -1, keepdims=True)
    i1 = jnp.min(jnp.where(le == v1, lanef, big), -1, keepdims=True)
    le2 = jnp.where(lanef == i1, neg, le)
    v2 = jnp.max(le2, -1, keepdims=True)
    i2 = jnp.min(jnp.where(le2 == v2, lanef, big), -1, keepdims=True)
    e2 = jnp.exp(v2 - v1)
    den = 1.0 + e2
    gate1 = (1.0 / den) * p_sel
    gate2 = (e2 / den) * p_sel
    info = jnp.where(lane == 0, i1 - n_groups,
                     jnp.where(lane == 1, i2 - n_groups,
                               jnp.where(lane == 2, gate1, jnp.where(lane == 3, gate2, 0.0))))
    info_ref[...] = info


def _router(h, g, w_route, b_route, n_groups, per_group):
    M, D = h.shape
    lane = jnp.arange(LANES, dtype=I32)
    is_expert = (lane >= n_groups) & (lane < n_groups * (1 + per_group))
    gid = jnp.where(is_expert, (lane - n_groups) // per_group, -1).astype(F32).reshape(1, LANES)
    return pl.pallas_call(
        functools.partial(_router_kernel, n_groups=n_groups),
        out_shape=[jax.ShapeDtypeStruct((M, D), F32), jax.ShapeDtypeStruct((M, LANES), F32)],
        grid=(M // ROW_TILE,),
        in_specs=[pl.BlockSpec((ROW_TILE, D), lambda i: (i, 0)),
                  pl.BlockSpec((1, D), lambda i: (0, 0)),
                  pl.BlockSpec((D, LANES), lambda i: (0, 0)),
                  pl.BlockSpec((1, LANES), lambda i: (0, 0)),
                  pl.BlockSpec((1, LANES), lambda i: (0, 0))],
        out_specs=[pl.BlockSpec((ROW_TILE, D), lambda i: (i, 0)),
                   pl.BlockSpec((ROW_TILE, LANES), lambda i: (i, 0))],
        scratch_shapes=[pltpu.VMEM((D, LANES), BF16), pltpu.VMEM((D, LANES), BF16)],
        compiler_params=_cparams("arbitrary"),
        name="moe_router",
    )(h, g.reshape(1, D), w_route, b_route, gid)


def _expert_kernel(te_ref, used_ref, tok_ref, first_ref, wslot_ref, nxt_ref, x_hbm, gate_ref,
                   wg_hbm, wu_hbm, wd_hbm, o_ref, xbuf, sem, wbuf_g, wbuf_u, wbuf_d, wsem, wgb, wub, wdb,
                   *, layer):
    i = pl.program_id(0)
    used = used_ref[0]
    slot = i % 2
    group_start = jnp.logical_and(i < used, first_ref[i] == 1)

    def weight_copies(e, ws):
        return [pltpu.make_async_copy(w.at[layer, e], buf.at[ws], wsem.at[n, ws])
                for n, (w, buf) in enumerate(((wg_hbm, wbuf_g), (wu_hbm, wbuf_u), (wd_hbm, wbuf_d)))]

    def row_copy(tile, r, sl):
        tok = tok_ref[tile * EXPERT_TILE + r]
        return pltpu.make_async_copy(x_hbm.at[pl.ds(tok, 1), :], xbuf.at[sl, pl.ds(r, 1), :], sem.at[sl])

    def start_tile(tile, sl):
        for r in range(EXPERT_TILE):
            row_copy(tile, r, sl).start()

    @pl.when(jnp.logical_and(i == 0, used > 0))
    def _():
        for c in weight_copies(te_ref[0], 0):
            c.start()
        start_tile(0, 0)

    @pl.when(i + 1 < used)
    def _():
        start_tile(i + 1, 1 - slot)

    @pl.when(group_start)
    def _():
        ws = wslot_ref[i]
        for c in weight_copies(te_ref[i], ws):
            c.wait()
        wgb[...] = wbuf_g[ws].astype(BF16)
        wub[...] = wbuf_u[ws].astype(BF16)
        wdb[...] = wbuf_d[ws].astype(BF16)

    @pl.when(jnp.logical_and(group_start, nxt_ref[i] >= 0))
    def _():
        for c in weight_copies(nxt_ref[i], 1 - wslot_ref[i]):
            c.start()

    @pl.when(i < used)
    def _():
        for r in range(EXPERT_TILE):
            row_copy(i, r, slot).wait()
        x = xbuf[slot].astype(BF16)
        hg = jnp.dot(x, wgb[...], preferred_element_type=F32)
        hu = jnp.dot(x, wub[...], preferred_element_type=F32)
        hid = (hg * jax.nn.sigmoid(hg)) * hu * gate_ref[...]
        o_ref[...] = jnp.dot(hid.astype(BF16), wdb[...], preferred_element_type=F32)

    @pl.when(i >= used)
    def _():
        o_ref[...] = jnp.zeros_like(o_ref)


def _experts(tile_expert, used, row_token, first, wslot, nxt, x, row_gate, w_gate, w_up, w_down, layer):
    R = row_token.shape[0]
    D = x.shape[1]
    _, E, _, Fd = w_gate.shape
    nt = R // EXPERT_TILE
    hbm = pl.BlockSpec(memory_space=pl.ANY)
    return pl.pallas_call(
        functools.partial(_expert_kernel, layer=layer),
        out_shape=jax.ShapeDtypeStruct((R, D), F32),
        grid_spec=pltpu.PrefetchScalarGridSpec(
            num_scalar_prefetch=6,
            grid=(nt,),
            in_specs=[hbm, pl.BlockSpec((EXPERT_TILE, 1), lambda i, *_: (i, 0)), hbm, hbm, hbm],
            out_specs=pl.BlockSpec((EXPERT_TILE, D), lambda i, *_: (i, 0)),
            scratch_shapes=[pltpu.VMEM((2, EXPERT_TILE, D), F32), pltpu.SemaphoreType.DMA((2,)),
                            pltpu.VMEM((2, D, Fd), F32), pltpu.VMEM((2, D, Fd), F32), pltpu.VMEM((2, Fd, D), F32),
                            pltpu.SemaphoreType.DMA((3, 2)),
                            pltpu.VMEM((D, Fd), BF16), pltpu.VMEM((D, Fd), BF16), pltpu.VMEM((Fd, D), BF16)],
        ),
        compiler_params=_cparams("arbitrary"),
        name="moe_experts",
    )(tile_expert, used, row_token, first, wslot, nxt, x, row_gate, w_gate, w_up, w_down)


def _moe(h, f_norm, w_group, b_group, w_expert, b_expert, w_gate, w_up, w_down, layer):
    M, D = h.shape
    G = w_group.shape[1]
    P = w_expert.shape[2]
    E = G * P
    assert G + E <= LANES
    w_route = jnp.zeros((D, LANES), F32)
    w_route = w_route.at[:, :G].set(w_group)
    w_route = w_route.at[:, G:G + E].set(jnp.transpose(w_expert, (1, 0, 2)).reshape(D, E))
    b_route = jnp.zeros((1, LANES), F32)
    b_route = b_route.at[0, :G].set(b_group)
    b_route = b_route.at[0, G:G + E].set(b_expert.reshape(E))
    xn, info = _router(h, f_norm, w_route, b_route, G, P)

    eid = info[:, :TOP_K].astype(I32).reshape(-1)
    gates = info[:, TOP_K:2 * TOP_K].reshape(-1)
    npair = M * TOP_K
    nt = -(-npair // EXPERT_TILE) + E
    R = nt * EXPERT_TILE
    onehot = (eid[:, None] == jnp.arange(E, dtype=I32)[None, :]).astype(I32)
    csum = jnp.cumsum(onehot, axis=0)
    counts = csum[-1]
    rank = jnp.sum(csum * onehot, axis=1) - 1
    padded = (counts + EXPERT_TILE - 1) // EXPERT_TILE * EXPERT_TILE
    pend = jnp.cumsum(padded)
    pstart = pend - padded
    dest = jnp.sum(onehot * pstart[None, :], axis=1) + rank
    token = (jnp.arange(npair, dtype=I32) // TOP_K).astype(F32)
    rows = jnp.zeros((R, 2), F32).at[dest].set(jnp.stack([token, gates], -1))
    row_token = rows[:, 0].astype(I32)
    tile_row0 = jnp.arange(nt, dtype=I32) * EXPERT_TILE
    tile_expert = jnp.minimum(jnp.sum((pend[None, :] <= tile_row0[:, None]).astype(I32), axis=1), E - 1)
    used = (pend[-1] // EXPERT_TILE).astype(I32).reshape(1)
    present = counts > 0
    e_ids = jnp.arange(E, dtype=I32)
    group_of = jnp.cumsum(present.astype(I32)) - 1
    later = jnp.where(present[None, :] & (e_ids[None, :] > e_ids[:, None]), e_ids[None, :], E)
    next_of = jnp.min(later, axis=1)
    next_of = jnp.where(next_of < E, next_of, -1).astype(I32)
    tile_valid = tile_row0 < pend[-1]
    prev_expert = jnp.concatenate([jnp.full((1,), -1, I32), tile_expert[:-1]])
    first = (tile_valid & (tile_expert != prev_expert)).astype(I32)
    wslot = (group_of[tile_expert] % 2).astype(I32)
    nxt = next_of[tile_expert]

    ys = _experts(tile_expert, used, row_token, first, wslot, nxt, xn, rows[:, 1:2], w_gate, w_up, w_down, layer)
    dest = dest.reshape(M, TOP_K)
    y = jnp.take(ys, dest[:, 0], axis=0)
    for s in range(1, TOP_K):
        y = y + jnp.take(ys, dest[:, s], axis=0)
    return h + y


def _row_reduce(x, combine, reduce):
    t = x[:, 0:LANES]
    for j in range(1, x.shape[1] // LANES):
        t = combine(t, x[:, j * LANES:(j + 1) * LANES])
    return reduce(t, -1, keepdims=True)


def _attn_prompt_kernel(qn_ref, qp_ref, kv_ref, wuk_ref, wuv_ref, o_ref, q_s, m_s, l_s, acc_s,
                        *, tq, tk, nh, dn, dc, scale):
    qi = pl.program_id(1)
    kv = pl.program_id(2)
    last = (qi * tq + tq - 1) // tk
    rows = nh * tq

    @pl.when(kv == 0)
    def _():
        for h in range(nh):
            qa = jnp.dot(qn_ref[:, h * dn:(h + 1) * dn], wuk_ref[h], preferred_element_type=F32) * scale
            q_s[h * tq:(h + 1) * tq, 0:dc] = qa.astype(BF16)
            q_s[h * tq:(h + 1) * tq, dc:] = qp_ref[:, h * LANES:(h + 1) * LANES]
        m_s[...] = jnp.full(m_s.shape, -jnp.inf, F32)
        l_s[...] = jnp.zeros(l_s.shape, F32)
        acc_s[...] = jnp.zeros(acc_s.shape, F32)

    def update(masked):
        kblk = kv_ref[...]
        s = lax.dot_general(q_s[...], kblk, (((1,), (1,)), ((), ())), preferred_element_type=F32)
        if masked:
            qpos = qi * tq + lax.broadcasted_iota(I32, (rows, tk), 0) % tq
            kpos = kv * tk + lax.broadcasted_iota(I32, (rows, tk), 1)
            s = jnp.where(kpos <= qpos, s, -jnp.inf)
        m_old = m_s[...]
        m_new = jnp.maximum(m_old, _row_reduce(s, jnp.maximum, jnp.max))
        alpha = jnp.exp(m_old - m_new)
        p = jnp.exp(s - m_new)
        l_s[...] = alpha * l_s[...] + _row_reduce(p, jnp.add, jnp.sum)
        acc_s[...] = alpha * acc_s[...] + jnp.dot(p.astype(BF16), kblk[:, 0:dc], preferred_element_type=F32)
        m_s[...] = m_new

    @pl.when(kv < last)
    def _():
        update(False)

    @pl.when(kv == last)
    def _():
        update(True)
        o = acc_s[...] / l_s[...]
        for h in range(nh):
            oh = jnp.dot(o[h * tq:(h + 1) * tq].astype(BF16), wuv_ref[h], preferred_element_type=F32)
            o_ref[:, h * oh.shape[1]:(h + 1) * oh.shape[1]] = oh.astype(o_ref.dtype)


def _attn_prompt(qn, qp, kvb, wuk_t, wuv_t, *, nb, seq, tq, tk, scale):
    nh, dn, dc = wuk_t.shape
    dv = wuv_t.shape[2]
    width = kvb.shape[1]
    assert width == dc + LANES and qp.shape[1] == nh * LANES
    nq, nk = seq // tq, seq // tk

    def kv_map(b, qi, kv):
        return (b * nk + jnp.minimum(kv, (qi * tq + tq - 1) // tk), 0)

    return pl.pallas_call(
        functools.partial(_attn_prompt_kernel, tq=tq, tk=tk, nh=nh, dn=dn, dc=dc, scale=scale),
        out_shape=jax.ShapeDtypeStruct((nb * seq, nh * dv), BF16),
        grid=(nb, nq, nk),
        in_specs=[pl.BlockSpec((tq, nh * dn), lambda b, qi, kv: (b * nq + qi, 0)),
                  pl.BlockSpec((tq, nh * LANES), lambda b, qi, kv: (b * nq + qi, 0)),
                  pl.BlockSpec((tk, width), kv_map),
                  pl.BlockSpec((nh, dn, dc), lambda b, qi, kv: (0, 0, 0)),
                  pl.BlockSpec((nh, dc, dv), lambda b, qi, kv: (0, 0, 0))],
        out_specs=pl.BlockSpec((tq, nh * dv), lambda b, qi, kv: (b * nq + qi, 0)),
        scratch_shapes=[pltpu.VMEM((nh * tq, width), BF16),
                        pltpu.VMEM((nh * tq, 1), F32),
                        pltpu.VMEM((nh * tq, 1), F32),
                        pltpu.VMEM((nh * tq, dc), F32)],
        compiler_params=_cparams("parallel", "parallel", "arbitrary"),
        name="mla_prompt_attn",
    )(qn, qp, kvb, wuk_t, wuv_t)


def _attn_decode_kernel(pt_ref, *refs, npg, dr, dc):
    qa_ref, qr_ref, kvn_ref = refs[:3]
    lat_refs = refs[3:3 + npg]
    kpe_refs = refs[3 + npg:3 + 2 * npg]
    o_ref = refs[3 + 2 * npg]
    m_s, l_s, acc_s, lat_s, kpe_s = refs[4 + 2 * npg:]
    g = pl.program_id(1)

    @pl.when(g == 0)
    def _():
        m_s[...] = jnp.full(m_s.shape, -jnp.inf, F32)
        l_s[...] = jnp.zeros(l_s.shape, F32)
        acc_s[...] = jnp.zeros(acc_s.shape, F32)

    qa = qa_ref[0]
    qr = qr_ref[0][:, 0:dr]
    nt = (((1,), (1,)), ((), ()))
    page = lat_refs[0].shape[1]
    for i in range(npg):
        lat_s[i * page:(i + 1) * page, :] = lat_refs[i][0].astype(BF16)
        kpe_s[:, i * page:(i + 1) * page] = kpe_refs[i][0].astype(BF16)
    lat = lat_s[...]
    s = (lax.dot_general(qa, lat, nt, preferred_element_type=F32)
         + jnp.dot(qr, kpe_s[...], preferred_element_type=F32))
    m_old = m_s[...]
    m_new = jnp.maximum(m_old, jnp.max(s, -1, keepdims=True))
    alpha = jnp.exp(m_old - m_new)
    p = jnp.exp(s - m_new)
    l_s[...] = alpha * l_s[...] + jnp.sum(p, -1, keepdims=True)
    acc_s[...] = alpha * acc_s[...] + jnp.dot(p.astype(BF16), lat, preferred_element_type=F32)
    m_s[...] = m_new

    @pl.when(g == pl.num_programs(1) - 1)
    def _():
        kvn = kvn_ref[0].astype(F32)
        s = (jnp.sum(qa.astype(F32) * kvn[:, 0:dc], -1, keepdims=True)
             + jnp.sum(qr.astype(F32) * kvn[:, dc:dc + dr], -1, keepdims=True))
        m_old = m_s[...]
        m_new = jnp.maximum(m_old, s)
        alpha = jnp.exp(m_old - m_new)
        p = jnp.exp(s - m_new)
        l = alpha * l_s[...] + p
        acc = alpha * acc_s[...] + p * kvn[:, 0:dc]
        o_ref[0] = (acc / l).astype(o_ref.dtype)


def _attn_decode(qa, qr, kvn, cache_latent, cache_kpe_t, page_table, *, dr, npg):
    nseq, npages = page_table.shape
    _, nh, dc = qa.shape
    width = kvn.shape[2]
    page = cache_latent.shape[1]
    assert npages % npg == 0

    def seq_spec(a, b):
        return pl.BlockSpec((1, a, b), lambda s, g, pt: (s, 0, 0))

    in_specs = [seq_spec(nh, dc), seq_spec(nh, LANES), seq_spec(1, width)]
    for i in range(npg):
        in_specs.append(pl.BlockSpec((1, page, dc), lambda s, g, pt, i=i: (pt[s * npages + g * npg + i], 0, 0)))
    for i in range(npg):
        in_specs.append(pl.BlockSpec((1, dr, page), lambda s, g, pt, i=i: (pt[s * npages + g * npg + i], 0, 0)))
    return pl.pallas_call(
        functools.partial(_attn_decode_kernel, npg=npg, dr=dr, dc=dc),
        out_shape=jax.ShapeDtypeStruct((nseq, nh, dc), BF16),
        grid_spec=pltpu.PrefetchScalarGridSpec(
            num_scalar_prefetch=1,
            grid=(nseq, npages // npg),
            in_specs=in_specs,
            out_specs=seq_spec(nh, dc),
            scratch_shapes=[pltpu.VMEM((nh, 1), F32), pltpu.VMEM((nh, 1), F32), pltpu.VMEM((nh, dc), F32),
                            pltpu.VMEM((npg * page, dc), BF16), pltpu.VMEM((dr, npg * page), BF16)],
        ),
        compiler_params=_cparams("parallel", "arbitrary"),
        name="mla_decode_attn",
    )(page_table.reshape(-1), qa, qr, kvn, *([cache_latent] * npg), *([cache_kpe_t] * npg))


def _head_mm_kernel(x_ref, w_ref, o_ref, *, scale):
    acc = jnp.dot(x_ref[...], w_ref[0], preferred_element_type=F32)
    if scale is not None:
        acc = acc * scale
    o_ref[...] = acc.astype(o_ref.dtype)


def _head_mm(x, w, scale=None):
    R = x.shape[0]
    nh, kin, kout = w.shape
    return pl.pallas_call(
        functools.partial(_head_mm_kernel, scale=scale),
        out_shape=jax.ShapeDtypeStruct((R, nh * kout), BF16),
        grid=(nh,),
        in_specs=[pl.BlockSpec((R, kin), lambda h: (0, h)),
                  pl.BlockSpec((1, kin, kout), lambda h: (h, 0, 0))],
        out_specs=pl.BlockSpec((R, kout), lambda h: (0, h)),
        compiler_params=_cparams("parallel"),
        name="head_mm",
    )(x, w)


def _rope_tables(pos, dr):
    half = dr // 2
    inv = ROPE_THETA ** (-2.0 * jnp.arange(half, dtype=F32) / dr)
    ang = pos.astype(F32)[:, None] * inv[None, :]
    cos, sin = jnp.cos(ang), jnp.sin(ang)
    rep = LANES // dr
    cos_t = jnp.tile(jnp.concatenate([cos, cos], -1), (1, rep))
    sin_t = jnp.tile(jnp.concatenate([-sin, sin], -1), (1, rep))
    return cos_t, sin_t


def _swap_halves(w, dr):
    return jnp.concatenate([w[..., dr // 2:], w[..., :dr // 2]], -1)


def _rwkv_layer(h, state_wkv, state_shift, n_prompt, nb, seq, aw, li):
    (a_norm, mu, w_r, w_k, w_v, w_o, w0, w1, w2, a0, a1, a2, g1, g2, k_k, k_a, r_k, lnx_w, lnx_b) = aw
    M, D = h.shape
    H = D // RWKV_HEAD
    nsample = M - n_prompt
    row = lambda x: ("row", x[li].reshape(1, D))
    xm, xn = _norm_shift_mix(h, a_norm[li], mu[li], state_shift, seq, n_prompt)
    jr, jw, jk, jv, ja, jg = range(6)

    r = _mm(xm, w_r, slab=jr, layer=li, name="rwkv_r")
    v = _mm(xm, w_v, slab=jv, layer=li, name="rwkv_v")
    th = _mm(xm, w1, slab=jw, layer=li, epilogue=jnp.tanh, out_dtypes=(BF16,), name="rwkv_w1")

    def decay_epilogue(z, w0r):
        u = -(w0r + z)
        softplus = jnp.maximum(u, 0.0) + jnp.log(1.0 + jnp.exp(-jnp.abs(u)))
        return jnp.exp(-jnp.exp(-softplus - 0.5))

    decay = _mm(th, w2, layer=li, extras=[row(w0)], epilogue=decay_epilogue, name="rwkv_w2")
    al = _mm(xm, a1, slab=ja, layer=li, out_dtypes=(BF16,), name="rwkv_a1")
    a_lr = _mm(al, a2, layer=li, extras=[row(a0)], name="rwkv_a2",
               epilogue=lambda z, a0r: jax.nn.sigmoid(a0r + z))
    gl = _mm(xm, g1, slab=jg, layer=li, epilogue=jax.nn.sigmoid, out_dtypes=(BF16,), name="rwkv_g1")
    g = _mm(gl, g2, layer=li, name="rwkv_g2")

    def k_epilogue(k, a, kkr, kar):
        seg = _seg_ones()
        kk = k * kkr
        nrm = jnp.sqrt(_seg_sum(kk * kk, seg))
        kk = kk / jnp.maximum(nrm, 1e-12)
        return k * (1.0 + (a - 1.0) * kar), -kk, kk * a

    k2, a_neg, b_pos = _mm(xm, w_k, slab=jk, layer=li, extras=[("full", a_lr), row(k_k), row(k_a)],
                           epilogue=k_epilogue, out_dtypes=(F32, F32, F32), name="rwkv_k")

    rows = (r, decay, k2, v, a_neg, b_pos)
    s0_p = jnp.zeros((nb, H // HEADS_PER_TILE * RWKV_HEAD, STATE_LANES), F32)
    yp_p, s_p = _wkv_scan_seq(rows, s0_p, nseq=nb, seq=seq, row0=0, nb=min(nb, 4), tc=min(seq, 32))
    yp_s, s_s = _wkv_scan_one(rows, _state_to_tiles(state_wkv), nseq=nsample, row0=n_prompt)
    y_parts = [jnp.concatenate([a.reshape(n_prompt, -1), b], 0) for a, b in zip(yp_p, yp_s)]

    yo = _wkv_post(y_parts, r, k2, v, g, lnx_w[li], lnx_b[li], r_k[li].reshape(D))
    h = _mm(yo, w_o, layer=li, extras=[("full", h)], epilogue=lambda acc, hh: hh + acc, name="rwkv_o")
    return h, _tiles_to_state(s_p, H), _tiles_to_state(s_s, H), xn


def _ple_layer(h, p_all, w_proj, g_norm, w_gate, li):
    pp = _mm(p_all, w_proj, slab=li, layer=li, name="ple_proj")
    return _mm(h, w_gate, layer=li, norm_gain=g_norm, extras=[("full", h), ("full", pp)], name="ple_gate",
               epilogue=lambda acc, hh, ppp: hh + ppp * jax.nn.sigmoid(acc))


def _shared_kv(h, pos, g_in, w_down, g_latent, dc, dr):
    M, D = h.shape
    cos_t, sin_t = _rope_tables(pos, dr)
    w_ext = jnp.concatenate([w_down, _swap_halves(w_down[:, dc:dc + dr], dr)], 1)

    assert 2 * dr == LANES

    def epilogue(acc, gl, ct, st):
        lat = _rms(acc[:, :dc], gl)
        t = acc[:, dc:]
        kpe = t * ct + pltpu.roll(t, dr, 1) * st
        out = jnp.concatenate([lat, kpe], 1)
        return out, out

    return _mm(h, w_ext, norm_gain=g_in,
               extras=[("const", g_latent.reshape(1, dc)), ("rows", cos_t), ("rows", sin_t)],
               epilogue=epilogue, out_dtypes=(F32, BF16), tn=dc + 2 * dr, name="kv_down")


def _mla_layer(h, pos, c_bf, n_prompt, nb, seq, cache_latent, cache_kpe_t, page_table, bw, li):
    b_norm, w_dq, g_q, w_uq, w_uk, w_uv, w_o = bw
    M, D = h.shape
    dc, nh, dn = w_uk.shape
    dv = w_uv.shape[2]
    ql = w_uq.shape[0]
    dr = w_uq.shape[1] // nh - dn
    scale = float(dn + dr) ** -0.5
    w_uq3 = w_uq.reshape(ql, nh, dn + dr)
    w_qn = w_uq3[:, :, :dn].reshape(ql, nh * dn)
    w_qr = w_uq3[:, :, dn:]
    assert 2 * dr == LANES
    w_qr_ext = jnp.concatenate([w_qr, _swap_halves(w_qr, dr)], -1).reshape(ql, nh * LANES)
    wuk_t = jnp.transpose(w_uk, (1, 2, 0)).astype(BF16)
    wuv_t = jnp.transpose(w_uv, (1, 0, 2)).astype(BF16)
    cos_t, sin_t = _rope_tables(pos, dr)

    cq = _mm(h, w_dq, layer=li, norm_gain=b_norm, extras=[("const", g_q.reshape(1, ql))], epilogue=_rms,
             out_dtypes=(BF16,), tn=ql, name="mla_dq")
    qn = _mm(cq, w_qn, out_dtypes=(BF16,), name="mla_uq_nope")

    def rope_epilogue(acc, ct, st):
        n = acc.shape[1]
        rot = acc * jnp.tile(ct, (1, nh)) + pltpu.roll(acc, n - dr, 1) * jnp.tile(st, (1, nh))
        keep = lax.broadcasted_iota(I32, acc.shape, 1) % LANES < dr
        return jnp.where(keep, rot * scale, 0.0)

    qp = _mm(cq, w_qr_ext, extras=[("rows", cos_t), ("rows", sin_t)], epilogue=rope_epilogue,
             out_dtypes=(BF16,), tn=nh * LANES, name="mla_uq_rope")

    o_p = _attn_prompt(qn, qp, c_bf, wuk_t, wuv_t, nb=nb, seq=seq, tq=min(seq, 128), tk=min(seq, 512),
                       scale=scale)
    ns = M - n_prompt
    qa_s = _head_mm(qn[n_prompt:], wuk_t, scale=scale).reshape(ns, nh, dc)
    ol_s = _attn_decode(qa_s, qp[n_prompt:].reshape(ns, nh, LANES), c_bf[n_prompt:].reshape(ns, 1, -1),
                        cache_latent, cache_kpe_t, page_table, dr=dr, npg=min(page_table.shape[1], 32))
    o_s = _head_mm(ol_s.reshape(ns, nh * dc), wuv_t)
    o = jnp.concatenate([o_p, o_s], 0)
    return _mm(o, w_o, layer=li, extras=[("full", h)], epilogue=lambda acc, hh: hh + acc, name="mla_o")


def kernel(x_prompt, x_sample, state_wkv, state_shift, cache_latent, cache_kpe, page_table, p_prompt, p_sample, a_norm, a_mu, a_wr, a_wk, a_wv, a_wo, a_w0, a_w1, a_w2, a_a0, a_a1, a_a2, a_g1, a_g2, a_kk, a_ka, a_rk, a_lnx_w, a_lnx_b, kv_norm, kv_wdown, kv_latent_norm, kv_wuk, kv_wuv, b_norm, b_wdq, b_qnorm, b_wuq, b_wo, f_norm, f_wgroup, f_bgroup, f_wexpert, f_bexpert, f_wgate, f_wup, f_wdown, pl_wproj, pl_norm, pl_wgate, final_norm):
    nb, seq, D = x_prompt.shape
    ns, dec_seq, _ = x_sample.shape
    assert dec_seq == 1
    depth = f_norm.shape[0]
    n_a = a_norm.shape[0]
    n_prompt = nb * seq
    dc = kv_latent_norm.shape[0]
    dr = kv_wdown.shape[1] - dc
    past_len = page_table.shape[1] * cache_latent.shape[1]
    pos = jnp.concatenate([jnp.tile(jnp.arange(seq, dtype=I32), nb), jnp.full((ns,), past_len, I32)])

    h = jnp.concatenate([x_prompt.reshape(n_prompt, D), x_sample.reshape(ns, D)], 0)
    p_all = jnp.concatenate([p_prompt.reshape(depth, n_prompt, -1), p_sample.reshape(depth, ns, -1)], 1)
    wkv_p, wkv_s, sh_p, sh_s = [], [], [], []
    c_f32 = c_bf = None
    cache_kpe_t = jnp.swapaxes(cache_kpe, 1, 2)
    aw = (a_norm, a_mu, a_wr, a_wk, a_wv, a_wo, a_w0, a_w1, a_w2, a_a0, a_a1, a_a2, a_g1, a_g2,
          a_kk, a_ka, a_rk, a_lnx_w, a_lnx_b)
    for i in range(depth):
        if i < n_a:
            h, s_p, s_s, xn = _rwkv_layer(h, state_wkv[i], state_shift[i], n_prompt, nb, seq, aw, i)
            wkv_p.append(s_p)
            wkv_s.append(s_s)
            sh_p.append(xn[seq - 1:n_prompt:seq])
            sh_s.append(xn[n_prompt:])
        else:
            j = i - n_a
            bw = (b_norm[j], b_wdq, b_qnorm[j], b_wuq[j], kv_wuk, kv_wuv, b_wo)
            h = _mla_layer(h, pos, c_bf, n_prompt, nb, seq, cache_latent, cache_kpe_t, page_table, bw, j)
        h = _moe(h, f_norm[i], f_wgroup[i], f_bgroup[i], f_wexpert[i], f_bexpert[i],
                 f_wgate, f_wup, f_wdown, i)
        h = _ple_layer(h, p_all, pl_wproj, pl_norm[i], pl_wgate, i)
        if i == n_a - 1:
            c_f32, c_bf = _shared_kv(h, pos, kv_norm, kv_wdown, kv_latent_norm, dc, dr)
    y = _norm(h, final_norm, F32)
    lat, kpe = c_f32[:, :dc], c_f32[:, dc:dc + dr]
    return (y[:n_prompt].reshape(nb, seq, D), y[n_prompt:].reshape(ns, 1, D),
            jnp.stack(wkv_p), jnp.stack(sh_p),
            lat[:n_prompt].reshape(nb, seq, dc), kpe[:n_prompt].reshape(nb, seq, dr),
            jnp.stack(wkv_s), jnp.stack(sh_s),
            lat[n_prompt:].reshape(ns, 1, dc), kpe[n_prompt:].reshape(ns, 1, dr))
```

```python
import functools

import jax
import jax.numpy as jnp
from jax import lax
from jax.experimental import pallas as pl
from jax.experimental.pallas import tpu as pltpu

F32 = jnp.float32
BF16 = jnp.bfloat16
I32 = jnp.int32

RMS_EPS = 1e-6
GN_EPS = 64e-5
ROPE_THETA = 10000.0
RWKV_HEAD = 64
LANES = 128
SUBLANES = 8
STATE_LANES = 256
HEADS_PER_TILE = STATE_LANES // RWKV_HEAD
ROW_TILE = 128
EXPERT_TILE = 128
TOP_K = 2
VMEM_LIMIT = 56 * 1024 * 1024


def _cparams(*sem):
    return pltpu.CompilerParams(dimension_semantics=sem, vmem_limit_bytes=VMEM_LIMIT)


def _pick_tile(n, target, mult=16):
    best = None
    for t in range(mult, min(n, target) + 1, mult):
        if n % t == 0:
            best = t
    assert best is not None, (n, target, mult)
    return best


def _rms(x, g):
    return x * lax.rsqrt(jnp.mean(x * x, -1, keepdims=True) + RMS_EPS) * g


def _split_bf16(x):
    hi = x.astype(BF16)
    lo = (x - hi.astype(F32)).astype(BF16)
    return hi, lo


def _seg_ones(width=LANES):
    r = lax.broadcasted_iota(I32, (width, width), 0) // RWKV_HEAD
    c = lax.broadcasted_iota(I32, (width, width), 1) // RWKV_HEAD
    return (r == c).astype(BF16)


def _seg_sum(x, seg):
    outs = []
    for j in range(x.shape[1] // LANES):
        hi, lo = _split_bf16(x[:, j * LANES:(j + 1) * LANES])
        outs.append(jnp.dot(hi, seg, preferred_element_type=F32)
                    + jnp.dot(lo, seg, preferred_element_type=F32))
    return jnp.concatenate(outs, axis=1)


def _mm_kernel(*refs, n_extra, n_out, epilogue, normed):
    x_ref, w_ref = refs[0], refs[1]
    extra_refs = refs[2:2 + n_extra]
    out_refs = refs[2 + n_extra:2 + n_extra + n_out]
    wb_ref = refs[2 + n_extra + n_out]

    @pl.when(pl.program_id(1) == 0)
    def _():
        wb_ref[...] = w_ref[...].astype(BF16)

    x = x_ref[...]
    if normed:
        x = _rms(x, extra_refs[-1][...])
        extra_refs = extra_refs[:-1]
    acc = jnp.dot(x.astype(BF16), wb_ref[...], preferred_element_type=F32)
    outs = epilogue(acc, *[r[...] for r in extra_refs]) if epilogue is not None else acc
    if not isinstance(outs, (tuple, list)):
        outs = (outs,)
    for o_ref, o in zip(out_refs, outs):
        o_ref[...] = o.astype(o_ref.dtype)


def _mm(x, w, *, slab=None, layer=None, norm_gain=None, epilogue=None, extras=(), out_dtypes=(F32,),
        out_cols=None, tn=512, tm_target=832, name="mm"):
    M, K = x.shape[-2:]
    N = w.shape[-1]
    tn = min(tn, N)
    assert N % tn == 0
    tm = _pick_tile(M, tm_target)
    extras = list(extras)
    if norm_gain is not None:
        extras.append(("const", norm_gain.reshape(1, K)))
    out_cols = out_cols or [tn] * len(out_dtypes)
    if x.ndim == 3:
        x_spec = pl.BlockSpec((None, tm, K), lambda j, i: (slab, i, 0))
    else:
        x_spec = pl.BlockSpec((tm, K), lambda j, i: (i, 0))
    if w.ndim == 3:
        w_spec = pl.BlockSpec((None, K, tn), lambda j, i: (layer, 0, j))
    else:
        w_spec = pl.BlockSpec((K, tn), lambda j, i: (0, j))
    in_specs = [x_spec, w_spec]
    args = [x, w]
    for kind, arr in extras:
        if kind == "row":
            in_specs.append(pl.BlockSpec((1, tn), lambda j, i: (0, j)))
        elif kind == "full":
            in_specs.append(pl.BlockSpec((tm, tn), lambda j, i: (i, j)))
        elif kind == "rows":
            in_specs.append(pl.BlockSpec((tm, arr.shape[1]), lambda j, i: (i, 0)))
        elif kind == "const":
            in_specs.append(pl.BlockSpec(arr.shape, lambda j, i: (0,) * arr.ndim))
        else:
            raise ValueError(kind)
        args.append(arr)
    nj = N // tn
    out_shape = [jax.ShapeDtypeStruct((M, oc * nj), dt) for dt, oc in zip(out_dtypes, out_cols)]
    out_specs = [pl.BlockSpec((tm, oc), lambda j, i: (i, j)) for oc in out_cols]
    res = pl.pallas_call(
        functools.partial(_mm_kernel, n_extra=len(extras), n_out=len(out_dtypes), epilogue=epilogue,
                          normed=norm_gain is not None),
        out_shape=out_shape,
        grid=(nj, M // tm),
        in_specs=in_specs,
        out_specs=out_specs,
        scratch_shapes=[pltpu.VMEM((K, tn), BF16)],
        compiler_params=_cparams("parallel", "arbitrary"),
        name=name,
    )(*args)
    return res[0] if len(res) == 1 else res


def _norm_kernel(h_ref, g_ref, o_ref):
    o_ref[...] = _rms(h_ref[...], g_ref[...]).astype(o_ref.dtype)


def _norm(h, g, out_dtype, row0=0, nrows=None):
    D = h.shape[1]
    M = h.shape[0] - row0 if nrows is None else nrows
    assert row0 % ROW_TILE == 0 and M % ROW_TILE == 0
    base = row0 // ROW_TILE
    return pl.pallas_call(
        _norm_kernel,
        out_shape=jax.ShapeDtypeStruct((M, D), out_dtype),
        grid=(M // ROW_TILE,),
        in_specs=[pl.BlockSpec((ROW_TILE, D), lambda i: (base + i, 0)),
                  pl.BlockSpec((1, D), lambda i: (0, 0))],
        out_specs=pl.BlockSpec((ROW_TILE, D), lambda i: (i, 0)),
        compiler_params=_cparams("parallel"),
        name="rmsnorm",
    )(h, g.reshape(1, D))


def _mix_kernel(h_ref, g_ref, mu_ref, sp_ref, xm_ref, xn_ref, carry_ref, *, tiles_per_seq, n_prompt_tiles):
    i = pl.program_id(0)
    xn = _rms(h_ref[...], g_ref[...])
    xn_ref[...] = xn
    prev = jnp.where(i % tiles_per_seq == 0, 0.0, carry_ref[...])
    row = lax.broadcasted_iota(I32, xn.shape, 0)
    shifted = jnp.where(row == 0, prev, pltpu.roll(xn, 1, 0))
    shifted = jnp.where(i >= n_prompt_tiles, sp_ref[...], shifted)
    carry_ref[...] = xn[ROW_TILE - 1:ROW_TILE, :]
    dx = shifted - xn
    for j in range(mu_ref.shape[0]):
        xm_ref[j] = (xn + dx * mu_ref[j:j + 1, :]).astype(BF16)


def _norm_shift_mix(h, g, mu, state_shift, seq, n_prompt):
    M, D = h.shape
    assert seq % ROW_TILE == 0 and n_prompt % ROW_TILE == 0 and (M - n_prompt) % ROW_TILE == 0
    npt = n_prompt // ROW_TILE
    nmix = mu.shape[0]
    return pl.pallas_call(
        functools.partial(_mix_kernel, tiles_per_seq=seq // ROW_TILE, n_prompt_tiles=npt),
        out_shape=[jax.ShapeDtypeStruct((nmix, M, D), BF16), jax.ShapeDtypeStruct((M, D), F32)],
        grid=(M // ROW_TILE,),
        in_specs=[pl.BlockSpec((ROW_TILE, D), lambda i: (i, 0)),
                  pl.BlockSpec((1, D), lambda i: (0, 0)),
                  pl.BlockSpec((nmix, D), lambda i: (0, 0)),
                  pl.BlockSpec((ROW_TILE, D), lambda i: (jnp.maximum(i - npt, 0), 0))],
        out_specs=[pl.BlockSpec((nmix, ROW_TILE, D), lambda i: (0, i, 0)),
                   pl.BlockSpec((ROW_TILE, D), lambda i: (i, 0))],
        scratch_shapes=[pltpu.VMEM((1, D), F32)],
        compiler_params=_cparams("arbitrary"),
        name="norm_shift_mix",
    )(h, g.reshape(1, D), mu, state_shift)


def _wkv_post_kernel(*refs):
    y_refs = refs[:HEADS_PER_TILE]
    r_ref, k_ref, v_ref, g_ref, lw_ref, lb_ref, rk_ref, o_ref = refs[HEADS_PER_TILE:]
    seg = _seg_ones()
    inv = 1.0 / RWKV_HEAD
    y = _heads_from_parts([y_ref[...] for y_ref in y_refs], r_ref.shape[1] // RWKV_HEAD)
    mean = _seg_sum(y, seg) * inv
    d = y - mean
    var = _seg_sum(d * d, seg) * inv
    yn = d * lax.rsqrt(var + GN_EPS) * lw_ref[...] + lb_ref[...]
    v = v_ref[...]
    bonus = _seg_sum(r_ref[...] * k_ref[...] * rk_ref[...], seg) * v
    o_ref[...] = ((yn + bonus) * g_ref[...]).astype(o_ref.dtype)


def _wkv_post(y_parts, r, k2, v, g, lnx_w, lnx_b, r_k):
    M, D = r.shape
    big = pl.BlockSpec((ROW_TILE, D), lambda i: (i, 0))
    part = pl.BlockSpec((ROW_TILE, D // HEADS_PER_TILE), lambda i: (i, 0))
    row = pl.BlockSpec((1, D), lambda i: (0, 0))
    return pl.pallas_call(
        _wkv_post_kernel,
        out_shape=jax.ShapeDtypeStruct((M, D), BF16),
        grid=(M // ROW_TILE,),
        in_specs=[part] * HEADS_PER_TILE + [big, big, big, big, row, row, row],
        out_specs=big,
        compiler_params=_cparams("parallel"),
        name="wkv_post",
    )(*y_parts, r, k2, v, g, lnx_w.reshape(1, D), lnx_b.reshape(1, D), r_k.reshape(1, D))


def _wkv_consts(ntile):
    seg = _seg_ones(STATE_LANES)
    rows = ntile * RWKV_HEAD
    rr = lax.broadcasted_iota(I32, (rows, STATE_LANES), 0) % RWKV_HEAD
    cc = lax.broadcasted_iota(I32, (rows, STATE_LANES), 1) % RWKV_HEAD
    diag = rr == cc
    eh = (lax.broadcasted_iota(I32, (SUBLANES, STATE_LANES), 0)
          == lax.broadcasted_iota(I32, (SUBLANES, STATE_LANES), 1) // RWKV_HEAD).astype(BF16)
    return seg, diag, eh


def _bcast_row(x, i, ntile):
    return jnp.concatenate(
        [jnp.broadcast_to(x[i:i + 1, j * STATE_LANES:(j + 1) * STATE_LANES], (RWKV_HEAD, STATE_LANES))
         for j in range(ntile)], axis=0)


def _wkv_step(S, vals, i, ntile, consts):
    r8, w8, k8, v8, a8, b8 = vals
    seg, diag, eh = consts
    sa = jnp.dot((S * _bcast_row(a8, i, ntile)).astype(BF16), seg, preferred_element_type=F32)
    vb = jnp.dot(jnp.where(diag, _bcast_row(v8, i, ntile), 0.0).astype(BF16), seg, preferred_element_type=F32)
    s_new = S * _bcast_row(w8, i, ntile) + sa * _bcast_row(b8, i, ntile) + vb * _bcast_row(k8, i, ntile)
    p = (s_new * _bcast_row(r8, i, ntile)).astype(BF16)
    y_t = lax.dot_general(eh, p, (((1,), (1,)), ((), ())), preferred_element_type=F32)
    return s_new, y_t


def _wkv_seq_kernel(*refs, nb, tc, ntile):
    in_refs = refs[:6 * nb]
    s0_ref = refs[6 * nb]
    y_refs = refs[6 * nb + 1:6 * nb + 1 + HEADS_PER_TILE]
    s_ref = refs[6 * nb + 1 + HEADS_PER_TILE]
    consts = _wkv_consts(ntile)

    @pl.when(pl.program_id(1) == 0)
    def _():
        s_ref[...] = s0_ref[...]

    def body(t8, carry):
        base = pl.multiple_of(t8 * SUBLANES, SUBLANES)
        vals = [[ref[pl.ds(base, SUBLANES), :] for ref in in_refs[6 * u:6 * u + 6]] for u in range(nb)]
        S = [s_ref[u] for u in range(nb)]
        ys = [[] for _ in range(nb)]
        for i in range(SUBLANES):
            for u in range(nb):
                S[u], y_t = _wkv_step(S[u], vals[u], i, ntile, consts)
                ys[u].append(y_t)
        for u in range(nb):
            s_ref[u] = S[u]
            for hs in range(HEADS_PER_TILE):
                y_refs[hs][u, pl.ds(base, SUBLANES), :] = jnp.concatenate(
                    [y[hs:hs + 1, :] for y in ys[u]], axis=0)
        return carry

    lax.fori_loop(0, tc // SUBLANES, body, 0)


def _wkv_scan_seq(rows, s0, *, nseq, seq, row0, nb, tc):
    D = rows[0].shape[1]
    ntile = D // STATE_LANES
    assert nseq % nb == 0 and seq % tc == 0 and row0 % tc == 0 and tc % SUBLANES == 0
    nt = seq // tc
    base = row0 // tc
    in_specs, args = [], []
    for u in range(nb):
        for arr in rows:
            in_specs.append(pl.BlockSpec((tc, D), lambda g, c, u=u: (base + (g * nb + u) * nt + c, 0)))
            args.append(arr)
    srows = ntile * RWKV_HEAD
    in_specs.append(pl.BlockSpec((nb, srows, STATE_LANES), lambda g, c: (g, 0, 0)))
    args.append(s0)
    half = D // HEADS_PER_TILE
    y_spec = pl.BlockSpec((nb, tc, half), lambda g, c: (g, c, 0))
    res = pl.pallas_call(
        functools.partial(_wkv_seq_kernel, nb=nb, tc=tc, ntile=ntile),
        out_shape=[jax.ShapeDtypeStruct((nseq, seq, half), F32)] * HEADS_PER_TILE
        + [jax.ShapeDtypeStruct((nseq, srows, STATE_LANES), F32)],
        grid=(nseq // nb, nt),
        in_specs=in_specs,
        out_specs=[y_spec] * HEADS_PER_TILE + [pl.BlockSpec((nb, srows, STATE_LANES), lambda g, c: (g, 0, 0))],
        compiler_params=_cparams("parallel", "arbitrary"),
        name="wkv_scan_seq",
    )(*args)
    return res[:HEADS_PER_TILE], res[HEADS_PER_TILE]


def _wkv_one_kernel(*refs, nu, ntile):
    in_refs = refs[:6]
    s0_ref = refs[6]
    y_refs = refs[7:7 + HEADS_PER_TILE]
    s_ref = refs[7 + HEADS_PER_TILE]
    consts = _wkv_consts(ntile)
    vals = [ref[...] for ref in in_refs]
    ys = []
    for u in range(nu):
        s_new, y_t = _wkv_step(s0_ref[u], vals, u, ntile, consts)
        s_ref[u] = s_new
        ys.append(y_t)
    for hs in range(HEADS_PER_TILE):
        y_refs[hs][...] = jnp.concatenate([y[hs:hs + 1, :] for y in ys], axis=0)


def _wkv_scan_one(rows, s0, *, nseq, row0, nu=SUBLANES):
    D = rows[0].shape[1]
    ntile = D // STATE_LANES
    assert nu == SUBLANES and nseq % nu == 0 and row0 % nu == 0
    base = row0 // nu
    srows = ntile * RWKV_HEAD
    half = D // HEADS_PER_TILE
    in_specs = [pl.BlockSpec((nu, D), lambda g: (base + g, 0)) for _ in rows]
    in_specs.append(pl.BlockSpec((nu, srows, STATE_LANES), lambda g: (g, 0, 0)))
    y_spec = pl.BlockSpec((nu, half), lambda g: (g, 0))
    res = pl.pallas_call(
        functools.partial(_wkv_one_kernel, nu=nu, ntile=ntile),
        out_shape=[jax.ShapeDtypeStruct((nseq, half), F32)] * HEADS_PER_TILE
        + [jax.ShapeDtypeStruct((nseq, srows, STATE_LANES), F32)],
        grid=(nseq // nu,),
        in_specs=in_specs,
        out_specs=[y_spec] * HEADS_PER_TILE + [pl.BlockSpec((nu, srows, STATE_LANES), lambda g: (g, 0, 0))],
        compiler_params=_cparams("parallel"),
        name="wkv_scan_one",
    )(*rows, s0)
    return res[:HEADS_PER_TILE], res[HEADS_PER_TILE]


def _state_to_tiles(s):
    n, H = s.shape[0], s.shape[1]
    s = s.reshape(n, H // HEADS_PER_TILE, HEADS_PER_TILE, RWKV_HEAD, RWKV_HEAD)
    return jnp.transpose(s, (0, 1, 3, 2, 4)).reshape(n, H // HEADS_PER_TILE * RWKV_HEAD, STATE_LANES)


def _tiles_to_state(s, H):
    n = s.shape[0]
    s = s.reshape(n, H // HEADS_PER_TILE, RWKV_HEAD, HEADS_PER_TILE, RWKV_HEAD)
    return jnp.transpose(s, (0, 1, 3, 2, 4)).reshape(n, H, RWKV_HEAD, RWKV_HEAD)


def _heads_from_parts(parts, H):
    return jnp.concatenate(
        [parts[h % HEADS_PER_TILE][:, (h // HEADS_PER_TILE) * RWKV_HEAD:(h // HEADS_PER_TILE + 1) * RWKV_HEAD]
         for h in range(H)], axis=1)


def _router_kernel(h_ref, g_ref, w_ref, b_ref, gid_ref, xb_ref, info_ref, wh_ref, wl_ref, *, n_groups):
    @pl.when(pl.program_id(0) == 0)
    def _():
        wh, wl = _split_bf16(w_ref[...])
        wh_ref[...] = wh
        wl_ref[...] = wl

    xn = _rms(h_ref[...], g_ref[...])
    xh, xl = _split_bf16(xn)
    xb_ref[...] = xn
    logits = (jnp.dot(xh, wh_ref[...], preferred_element_type=F32)
              + jnp.dot(xl, wh_ref[...], preferred_element_type=F32)
              + jnp.dot(xh, wl_ref[...], preferred_element_type=F32)) + b_ref[...]
    lane = lax.broadcasted_iota(I32, logits.shape, 1)
    lanef = lane.astype(F32)
    neg = -jnp.inf
    big = float(LANES)
    lg = jnp.where(lane < n_groups, logits, neg)
    mg = jnp.max(lg, -1, keepdims=True)
    g_top = jnp.min(jnp.where(lg == mg, lanef, big), -1, keepdims=True)
    p_sel = 1.0 / jnp.sum(jnp.exp(lg - mg), -1, keepdims=True)
    le = jnp.where(gid_ref[...] == g_top, logits, neg)
    v1 = jnp.max(le, -1, keepdims=True)
    i1 = jnp.min(jnp.where(le == v1, lanef, big), -1, keepdims=True)
    le2 = jnp.where(lanef == i1, neg, le)
    v2 = jnp.max(le2, -1, keepdims=True)
    i2 = jnp.min(jnp.where(le2 == v2, lanef, big), -1, keepdims=True)
    e2 = jnp.exp(v2 - v1)
    den = 1.0 + e2
    gate1 = (1.0 / den) * p_sel
    gate2 = (e2 / den) * p_sel
    info = jnp.where(lane == 0, i1 - n_groups,
                     jnp.where(lane == 1, i2 - n_groups,
                               jnp.where(lane == 2, gate1, jnp.where(lane == 3, gate2, 0.0))))
    info_ref[...] = info


def _router(h, g, w_route, b_route, n_groups, per_group):
    M, D = h.shape
    lane = jnp.arange(LANES, dtype=I32)
    is_expert = (lane >= n_groups) & (lane < n_groups * (1 + per_group))
    gid = jnp.where(is_expert, (lane - n_groups) // per_group, -1).astype(F32).reshape(1, LANES)
    return pl.pallas_call(
        functools.partial(_router_kernel, n_groups=n_groups),
        out_shape=[jax.ShapeDtypeStruct((M, D), F32), jax.ShapeDtypeStruct((M, LANES), F32)],
        grid=(M // ROW_TILE,),
        in_specs=[pl.BlockSpec((ROW_TILE, D), lambda i: (i, 0)),
                  pl.BlockSpec((1, D), lambda i: (0, 0)),
                  pl.BlockSpec((D, LANES), lambda i: (0, 0)),
                  pl.BlockSpec((1, LANES), lambda i: (0, 0)),
                  pl.BlockSpec((1, LANES), lambda i: (0, 0))],
        out_specs=[pl.BlockSpec((ROW_TILE, D), lambda i: (i, 0)),
                   pl.BlockSpec((ROW_TILE, LANES), lambda i: (i, 0))],
        scratch_shapes=[pltpu.VMEM((D, LANES), BF16), pltpu.VMEM((D, LANES), BF16)],
        compiler_params=_cparams("arbitrary"),
        name="moe_router",
    )(h, g.reshape(1, D), w_route, b_route, gid)


def _expert_kernel(te_ref, used_ref, tok_ref, x_hbm, gate_ref, wg_ref, wu_ref, wd_ref, o_ref,
                   xbuf, sem, wgb, wub, wdb):
    i = pl.program_id(0)
    used = used_ref[0]
    slot = i % 2
    new_expert = jnp.logical_or(i == 0, te_ref[i] != te_ref[jnp.maximum(i - 1, 0)])

    def row_copy(tile, r, sl):
        tok = tok_ref[tile * EXPERT_TILE + r]
        return pltpu.make_async_copy(x_hbm.at[pl.ds(tok, 1), :], xbuf.at[sl, pl.ds(r, 1), :], sem.at[sl])

    def start_tile(tile, sl):
        for r in range(EXPERT_TILE):
            row_copy(tile, r, sl).start()

    @pl.when(jnp.logical_and(i == 0, used > 0))
    def _():
        start_tile(0, 0)

    @pl.when(i + 1 < used)
    def _():
        start_tile(i + 1, 1 - slot)

    @pl.when(jnp.logical_and(i < used, new_expert))
    def _():
        wgb[...] = wg_ref[0].astype(BF16)
        wub[...] = wu_ref[0].astype(BF16)
        wdb[...] = wd_ref[0].astype(BF16)

    @pl.when(i < used)
    def _():
        for r in range(EXPERT_TILE):
            row_copy(i, r, slot).wait()
        x = xbuf[slot].astype(BF16)
        hg = jnp.dot(x, wgb[...], preferred_element_type=F32)
        hu = jnp.dot(x, wub[...], preferred_element_type=F32)
        hid = (hg * jax.nn.sigmoid(hg)) * hu * gate_ref[...]
        o_ref[...] = jnp.dot(hid.astype(BF16), wdb[...], preferred_element_type=F32)

    @pl.when(i >= used)
    def _():
        o_ref[...] = jnp.zeros_like(o_ref)


def _experts(tile_expert, used, row_token, x, row_gate, w_gate, w_up, w_down, layer):
    R = row_token.shape[0]
    D = x.shape[1]
    _, E, _, Fd = w_gate.shape
    nt = R // EXPERT_TILE
    return pl.pallas_call(
        _expert_kernel,
        out_shape=jax.ShapeDtypeStruct((R, D), F32),
        grid_spec=pltpu.PrefetchScalarGridSpec(
            num_scalar_prefetch=3,
            grid=(nt,),
            in_specs=[pl.BlockSpec(memory_space=pl.ANY),
                      pl.BlockSpec((EXPERT_TILE, 1), lambda i, te, u, tk: (i, 0)),
                      pl.BlockSpec((None, 1, D, Fd), lambda i, te, u, tk: (layer, te[i], 0, 0)),
                      pl.BlockSpec((None, 1, D, Fd), lambda i, te, u, tk: (layer, te[i], 0, 0)),
                      pl.BlockSpec((None, 1, Fd, D), lambda i, te, u, tk: (layer, te[i], 0, 0))],
            out_specs=pl.BlockSpec((EXPERT_TILE, D), lambda i, te, u, tk: (i, 0)),
            scratch_shapes=[pltpu.VMEM((2, EXPERT_TILE, D), F32), pltpu.SemaphoreType.DMA((2,)),
                            pltpu.VMEM((D, Fd), BF16), pltpu.VMEM((D, Fd), BF16), pltpu.VMEM((Fd, D), BF16)],
        ),
        compiler_params=_cparams("arbitrary"),
        name="moe_experts",
    )(tile_expert, used, row_token, x, row_gate, w_gate, w_up, w_down)


def _moe(h, f_norm, w_group, b_group, w_expert, b_expert, w_gate, w_up, w_down, layer):
    M, D = h.shape
    G = w_group.shape[1]
    P = w_expert.shape[2]
    E = G * P
    assert G + E <= LANES
    w_route = jnp.zeros((D, LANES), F32)
    w_route = w_route.at[:, :G].set(w_group)
    w_route = w_route.at[:, G:G + E].set(jnp.transpose(w_expert, (1, 0, 2)).reshape(D, E))
    b_route = jnp.zeros((1, LANES), F32)
    b_route = b_route.at[0, :G].set(b_group)
    b_route = b_route.at[0, G:G + E].set(b_expert.reshape(E))
    xn, info = _router(h, f_norm, w_route, b_route, G, P)

    eid = info[:, :TOP_K].astype(I32).reshape(-1)
    gates = info[:, TOP_K:2 * TOP_K].reshape(-1)
    npair = M * TOP_K
    nt = -(-npair // EXPERT_TILE) + E
    R = nt * EXPERT_TILE
    onehot = (eid[:, None] == jnp.arange(E, dtype=I32)[None, :]).astype(I32)
    csum = jnp.cumsum(onehot, axis=0)
    counts = csum[-1]
    rank = jnp.sum(csum * onehot, axis=1) - 1
    padded = (counts + EXPERT_TILE - 1) // EXPERT_TILE * EXPERT_TILE
    pend = jnp.cumsum(padded)
    pstart = pend - padded
    dest = jnp.sum(onehot * pstart[None, :], axis=1) + rank
    token = (jnp.arange(npair, dtype=I32) // TOP_K).astype(F32)
    rows = jnp.zeros((R, 2), F32).at[dest].set(jnp.stack([token, gates], -1))
    row_token = rows[:, 0].astype(I32)
    tile_row0 = jnp.arange(nt, dtype=I32) * EXPERT_TILE
    tile_expert = jnp.minimum(jnp.sum((pend[None, :] <= tile_row0[:, None]).astype(I32), axis=1), E - 1)
    used = (pend[-1] // EXPERT_TILE).astype(I32).reshape(1)

    ys = _experts(tile_expert, used, row_token, xn, rows[:, 1:2], w_gate, w_up, w_down, layer)
    dest = dest.reshape(M, TOP_K)
    y = jnp.take(ys, dest[:, 0], axis=0)
    for s in range(1, TOP_K):
        y = y + jnp.take(ys, dest[:, s], axis=0)
    return h + y


def _row_reduce(x, combine, reduce):
    t = x[:, 0:LANES]
    for j in range(1, x.shape[1] // LANES):
        t = combine(t, x[:, j * LANES:(j + 1) * LANES])
    return reduce(t, -1, keepdims=True)


def _attn_prompt_kernel(qn_ref, qp_ref, kv_ref, wuk_ref, wuv_ref, o_ref, q_s, m_s, l_s, acc_s,
                        *, tq, tk, nh, dn, dc, scale):
    qi = pl.program_id(1)
    kv = pl.program_id(2)
    last = (qi * tq + tq - 1) // tk
    rows = nh * tq

    @pl.when(kv == 0)
    def _():
        for h in range(nh):
            qa = jnp.dot(qn_ref[:, h * dn:(h + 1) * dn], wuk_ref[h], preferred_element_type=F32) * scale
            q_s[h * tq:(h + 1) * tq, 0:dc] = qa.astype(BF16)
            q_s[h * tq:(h + 1) * tq, dc:] = qp_ref[:, h * LANES:(h + 1) * LANES]
        m_s[...] = jnp.full(m_s.shape, -jnp.inf, F32)
        l_s[...] = jnp.zeros(l_s.shape, F32)
        acc_s[...] = jnp.zeros(acc_s.shape, F32)

    def update(masked):
        kblk = kv_ref[...]
        s = lax.dot_general(q_s[...], kblk, (((1,), (1,)), ((), ())), preferred_element_type=F32)
        if masked:
            qpos = qi * tq + lax.broadcasted_iota(I32, (rows, tk), 0) % tq
            kpos = kv * tk + lax.broadcasted_iota(I32, (rows, tk), 1)
            s = jnp.where(kpos <= qpos, s, -jnp.inf)
        m_old = m_s[...]
        m_new = jnp.maximum(m_old, _row_reduce(s, jnp.maximum, jnp.max))
        alpha = jnp.exp(m_old - m_new)
        p = jnp.exp(s - m_new)
        l_s[...] = alpha * l_s[...] + _row_reduce(p, jnp.add, jnp.sum)
        acc_s[...] = alpha * acc_s[...] + jnp.dot(p.astype(BF16), kblk[:, 0:dc], preferred_element_type=F32)
        m_s[...] = m_new

    @pl.when(kv < last)
    def _():
        update(False)

    @pl.when(kv == last)
    def _():
        update(True)
        o = acc_s[...] / l_s[...]
        for h in range(nh):
            oh = jnp.dot(o[h * tq:(h + 1) * tq].astype(BF16), wuv_ref[h], preferred_element_type=F32)
            o_ref[:, h * oh.shape[1]:(h + 1) * oh.shape[1]] = oh.astype(o_ref.dtype)


def _attn_prompt(qn, qp, kvb, wuk_t, wuv_t, *, nb, seq, tq, tk, scale):
    nh, dn, dc = wuk_t.shape
    dv = wuv_t.shape[2]
    width = kvb.shape[1]
    assert width == dc + LANES and qp.shape[1] == nh * LANES
    nq, nk = seq // tq, seq // tk

    def kv_map(b, qi, kv):
        return (b * nk + jnp.minimum(kv, (qi * tq + tq - 1) // tk), 0)

    return pl.pallas_call(
        functools.partial(_attn_prompt_kernel, tq=tq, tk=tk, nh=nh, dn=dn, dc=dc, scale=scale),
        out_shape=jax.ShapeDtypeStruct((nb * seq, nh * dv), BF16),
        grid=(nb, nq, nk),
        in_specs=[pl.BlockSpec((tq, nh * dn), lambda b, qi, kv: (b * nq + qi, 0)),
                  pl.BlockSpec((tq, nh * LANES), lambda b, qi, kv: (b * nq + qi, 0)),
                  pl.BlockSpec((tk, width), kv_map),
                  pl.BlockSpec((nh, dn, dc), lambda b, qi, kv: (0, 0, 0)),
                  pl.BlockSpec((nh, dc, dv), lambda b, qi, kv: (0, 0, 0))],
        out_specs=pl.BlockSpec((tq, nh * dv), lambda b, qi, kv: (b * nq + qi, 0)),
        scratch_shapes=[pltpu.VMEM((nh * tq, width), BF16),
                        pltpu.VMEM((nh * tq, 1), F32),
                        pltpu.VMEM((nh * tq, 1), F32),
                        pltpu.VMEM((nh * tq, dc), F32)],
        compiler_params=_cparams("parallel", "parallel", "arbitrary"),
        name="mla_prompt_attn",
    )(qn, qp, kvb, wuk_t, wuv_t)


def _attn_decode_kernel(pt_ref, *refs, npg, dr, dc):
    qa_ref, qr_ref, kvn_ref = refs[:3]
    lat_hbm, kpe_hbm, o_ref, m_s, l_s, acc_s, lat_s, kpe_s, latbuf, kpebuf, sem = refs[3:]
    g = pl.program_id(1)
    ng = pl.num_programs(1)
    t = pl.program_id(0) * ng + g
    total = pl.num_programs(0) * ng
    slot = t % 2

    def copies(step, sl):
        out = []
        for i in range(npg):
            pg = pt_ref[step * npg + i]
            out.append(pltpu.make_async_copy(lat_hbm.at[pg], latbuf.at[sl, i], sem.at[0, sl]))
            out.append(pltpu.make_async_copy(kpe_hbm.at[pg], kpebuf.at[sl, i], sem.at[1, sl]))
        return out

    @pl.when(t == 0)
    def _():
        for c in copies(0, 0):
            c.start()

    nxt = jnp.minimum(t + 1, total - 1)
    for c in copies(nxt, 1 - slot):
        c.start()
    for c in copies(t, slot):
        c.wait()

    @pl.when(g == 0)
    def _():
        m_s[...] = jnp.full(m_s.shape, -jnp.inf, F32)
        l_s[...] = jnp.zeros(l_s.shape, F32)
        acc_s[...] = jnp.zeros(acc_s.shape, F32)

    qa = qa_ref[0]
    qr = qr_ref[0][:, 0:dr]
    nt = (((1,), (1,)), ((), ()))
    page = latbuf.shape[2]
    for i in range(npg):
        lat_s[i * page:(i + 1) * page, :] = latbuf[slot, i].astype(BF16)
        kpe_s[:, i * page:(i + 1) * page] = kpebuf[slot, i].astype(BF16)
    lat = lat_s[...]
    s = (lax.dot_general(qa, lat, nt, preferred_element_type=F32)
         + jnp.dot(qr, kpe_s[...], preferred_element_type=F32))
    m_old = m_s[...]
    m_new = jnp.maximum(m_old, jnp.max(s, -1, keepdims=True))
    alpha = jnp.exp(m_old - m_new)
    p = jnp.exp(s - m_new)
    l_s[...] = alpha * l_s[...] + jnp.sum(p, -1, keepdims=True)
    acc_s[...] = alpha * acc_s[...] + jnp.dot(p.astype(BF16), lat, preferred_element_type=F32)
    m_s[...] = m_new

    @pl.when(g == pl.num_programs(1) - 1)
    def _():
        kvn = kvn_ref[0].astype(F32)
        s = (jnp.sum(qa.astype(F32) * kvn[:, 0:dc], -1, keepdims=True)
             + jnp.sum(qr.astype(F32) * kvn[:, dc:dc + dr], -1, keepdims=True))
        m_old = m_s[...]
        m_new = jnp.maximum(m_old, s)
        alpha = jnp.exp(m_old - m_new)
        p = jnp.exp(s - m_new)
        l = alpha * l_s[...] + p
        acc = alpha * acc_s[...] + p * kvn[:, 0:dc]
        o_ref[0] = (acc / l).astype(o_ref.dtype)

    @pl.when(t == total - 1)
    def _():
        for c in copies(nxt, 1 - slot):
            c.wait()


def _attn_decode(qa, qr, kvn, cache_latent, cache_kpe_t, page_table, *, dr, npg):
    nseq, npages = page_table.shape
    _, nh, dc = qa.shape
    width = kvn.shape[2]
    page = cache_latent.shape[1]
    assert npages % npg == 0

    def seq_spec(a, b):
        return pl.BlockSpec((1, a, b), lambda s, g, pt: (s, 0, 0))

    hbm = pl.BlockSpec(memory_space=pl.ANY)
    in_specs = [seq_spec(nh, dc), seq_spec(nh, LANES), seq_spec(1, width), hbm, hbm]
    return pl.pallas_call(
        functools.partial(_attn_decode_kernel, npg=npg, dr=dr, dc=dc),
        out_shape=jax.ShapeDtypeStruct((nseq, nh, dc), BF16),
        grid_spec=pltpu.PrefetchScalarGridSpec(
            num_scalar_prefetch=1,
            grid=(nseq, npages // npg),
            in_specs=in_specs,
            out_specs=seq_spec(nh, dc),
            scratch_shapes=[pltpu.VMEM((nh, 1), F32), pltpu.VMEM((nh, 1), F32), pltpu.VMEM((nh, dc), F32),
                            pltpu.VMEM((npg * page, dc), BF16), pltpu.VMEM((dr, npg * page), BF16),
                            pltpu.VMEM((2, npg, page, dc), F32), pltpu.VMEM((2, npg, dr, page), F32),
                            pltpu.SemaphoreType.DMA((2, 2))],
        ),
        compiler_params=_cparams("arbitrary", "arbitrary"),
        name="mla_decode_attn",
    )(page_table.reshape(-1), qa, qr, kvn, cache_latent, cache_kpe_t)


def _head_mm_kernel(x_ref, w_ref, o_ref, *, scale):
    acc = jnp.dot(x_ref[...], w_ref[0], preferred_element_type=F32)
    if scale is not None:
        acc = acc * scale
    o_ref[...] = acc.astype(o_ref.dtype)


def _head_mm(x, w, scale=None):
    R = x.shape[0]
    nh, kin, kout = w.shape
    return pl.pallas_call(
        functools.partial(_head_mm_kernel, scale=scale),
        out_shape=jax.ShapeDtypeStruct((R, nh * kout), BF16),
        grid=(nh,),
        in_specs=[pl.BlockSpec((R, kin), lambda h: (0, h)),
                  pl.BlockSpec((1, kin, kout), lambda h: (h, 0, 0))],
        out_specs=pl.BlockSpec((R, kout), lambda h: (0, h)),
        compiler_params=_cparams("parallel"),
        name="head_mm",
    )(x, w)


def _rope_tables(pos, dr):
    half = dr // 2
    inv = ROPE_THETA ** (-2.0 * jnp.arange(half, dtype=F32) / dr)
    ang = pos.astype(F32)[:, None] * inv[None, :]
    cos, sin = jnp.cos(ang), jnp.sin(ang)
    rep = LANES // dr
    cos_t = jnp.tile(jnp.concatenate([cos, cos], -1), (1, rep))
    sin_t = jnp.tile(jnp.concatenate([-sin, sin], -1), (1, rep))
    return cos_t, sin_t


def _swap_halves(w, dr):
    return jnp.concatenate([w[..., dr // 2:], w[..., :dr // 2]], -1)


def _rwkv_layer(h, state_wkv, state_shift, n_prompt, nb, seq, aw, li):
    (a_norm, mu, w_r, w_k, w_v, w_o, w0, w1, w2, a0, a1, a2, g1, g2, k_k, k_a, r_k, lnx_w, lnx_b) = aw
    M, D = h.shape
    H = D // RWKV_HEAD
    nsample = M - n_prompt
    row = lambda x: ("row", x[li].reshape(1, D))
    xm, xn = _norm_shift_mix(h, a_norm[li], mu[li], state_shift, seq, n_prompt)
    jr, jw, jk, jv, ja, jg = range(6)

    r = _mm(xm, w_r, slab=jr, layer=li, name="rwkv_r")
    v = _mm(xm, w_v, slab=jv, layer=li, name="rwkv_v")
    th = _mm(xm, w1, slab=jw, layer=li, epilogue=jnp.tanh, out_dtypes=(BF16,), name="rwkv_w1")

    def decay_epilogue(z, w0r):
        u = -(w0r + z)
        softplus = jnp.maximum(u, 0.0) + jnp.log(1.0 + jnp.exp(-jnp.abs(u)))
        return jnp.exp(-jnp.exp(-softplus - 0.5))

    decay = _mm(th, w2, layer=li, extras=[row(w0)], epilogue=decay_epilogue, name="rwkv_w2")
    al = _mm(xm, a1, slab=ja, layer=li, out_dtypes=(BF16,), name="rwkv_a1")
    a_lr = _mm(al, a2, layer=li, extras=[row(a0)], name="rwkv_a2",
               epilogue=lambda z, a0r: jax.nn.sigmoid(a0r + z))
    gl = _mm(xm, g1, slab=jg, layer=li, epilogue=jax.nn.sigmoid, out_dtypes=(BF16,), name="rwkv_g1")
    g = _mm(gl, g2, layer=li, name="rwkv_g2")

    def k_epilogue(k, a, kkr, kar):
        seg = _seg_ones()
        kk = k * kkr
        nrm = jnp.sqrt(_seg_sum(kk * kk, seg))
        kk = kk / jnp.maximum(nrm, 1e-12)
        return k * (1.0 + (a - 1.0) * kar), -kk, kk * a

    k2, a_neg, b_pos = _mm(xm, w_k, slab=jk, layer=li, extras=[("full", a_lr), row(k_k), row(k_a)],
                           epilogue=k_epilogue, out_dtypes=(F32, F32, F32), name="rwkv_k")

    rows = (r, decay, k2, v, a_neg, b_pos)
    s0_p = jnp.zeros((nb, H // HEADS_PER_TILE * RWKV_HEAD, STATE_LANES), F32)
    yp_p, s_p = _wkv_scan_seq(rows, s0_p, nseq=nb, seq=seq, row0=0, nb=min(nb, 4), tc=min(seq, 32))
    yp_s, s_s = _wkv_scan_one(rows, _state_to_tiles(state_wkv), nseq=nsample, row0=n_prompt)
    y_parts = [jnp.concatenate([a.reshape(n_prompt, -1), b], 0) for a, b in zip(yp_p, yp_s)]

    yo = _wkv_post(y_parts, r, k2, v, g, lnx_w[li], lnx_b[li], r_k[li].reshape(D))
    h = _mm(yo, w_o, layer=li, extras=[("full", h)], epilogue=lambda acc, hh: hh + acc, name="rwkv_o")
    return h, _tiles_to_state(s_p, H), _tiles_to_state(s_s, H), xn


def _ple_layer(h, p_all, w_proj, g_norm, w_gate, li):
    pp = _mm(p_all, w_proj, slab=li, layer=li, name="ple_proj")
    return _mm(h, w_gate, layer=li, norm_gain=g_norm, extras=[("full", h), ("full", pp)], name="ple_gate",
               epilogue=lambda acc, hh, ppp: hh + ppp * jax.nn.sigmoid(acc))


def _shared_kv(h, pos, g_in, w_down, g_latent, dc, dr):
    M, D = h.shape
    cos_t, sin_t = _rope_tables(pos, dr)
    w_ext = jnp.concatenate([w_down, _swap_halves(w_down[:, dc:dc + dr], dr)], 1)

    assert 2 * dr == LANES

    def epilogue(acc, gl, ct, st):
        lat = _rms(acc[:, :dc], gl)
        t = acc[:, dc:]
        kpe = t * ct + pltpu.roll(t, dr, 1) * st
        out = jnp.concatenate([lat, kpe], 1)
        return out, out

    return _mm(h, w_ext, norm_gain=g_in,
               extras=[("const", g_latent.reshape(1, dc)), ("rows", cos_t), ("rows", sin_t)],
               epilogue=epilogue, out_dtypes=(F32, BF16), tn=dc + 2 * dr, name="kv_down")


def _mla_layer(h, pos, c_bf, n_prompt, nb, seq, cache_latent, cache_kpe_t, page_table, bw, li):
    b_norm, w_dq, g_q, w_uq, w_uk, w_uv, w_o = bw
    M, D = h.shape
    dc, nh, dn = w_uk.shape
    dv = w_uv.shape[2]
    ql = w_uq.shape[0]
    dr = w_uq.shape[1] // nh - dn
    scale = float(dn + dr) ** -0.5
    w_uq3 = w_uq.reshape(ql, nh, dn + dr)
    w_qn = w_uq3[:, :, :dn].reshape(ql, nh * dn)
    w_qr = w_uq3[:, :, dn:]
    assert 2 * dr == LANES
    w_qr_ext = jnp.concatenate([w_qr, _swap_halves(w_qr, dr)], -1).reshape(ql, nh * LANES)
    wuk_t = jnp.transpose(w_uk, (1, 2, 0)).astype(BF16)
    wuv_t = jnp.transpose(w_uv, (1, 0, 2)).astype(BF16)
    cos_t, sin_t = _rope_tables(pos, dr)

    cq = _mm(h, w_dq, layer=li, norm_gain=b_norm, extras=[("const", g_q.reshape(1, ql))], epilogue=_rms,
             out_dtypes=(BF16,), tn=ql, name="mla_dq")
    qn = _mm(cq, w_qn, out_dtypes=(BF16,), name="mla_uq_nope")

    def rope_epilogue(acc, ct, st):
        n = acc.shape[1]
        rot = acc * jnp.tile(ct, (1, nh)) + pltpu.roll(acc, n - dr, 1) * jnp.tile(st, (1, nh))
        keep = lax.broadcasted_iota(I32, acc.shape, 1) % LANES < dr
        return jnp.where(keep, rot * scale, 0.0)

    qp = _mm(cq, w_qr_ext, extras=[("rows", cos_t), ("rows", sin_t)], epilogue=rope_epilogue,
             out_dtypes=(BF16,), tn=nh * LANES, name="mla_uq_rope")

    o_p = _attn_prompt(qn, qp, c_bf, wuk_t, wuv_t, nb=nb, seq=seq, tq=min(seq, 128), tk=min(seq, 512),
                       scale=scale)
    ns = M - n_prompt
    qa_s = _head_mm(qn[n_prompt:], wuk_t, scale=scale).reshape(ns, nh, dc)
    ol_s = _attn_decode(qa_s, qp[n_prompt:].reshape(ns, nh, LANES), c_bf[n_prompt:].reshape(ns, 1, -1),
                        cache_latent, cache_kpe_t, page_table, dr=dr, npg=min(page_table.shape[1], 32))
    o_s = _head_mm(ol_s.reshape(ns, nh * dc), wuv_t)
    o = jnp.concatenate([o_p, o_s], 0)
    return _mm(o, w_o, layer=li, extras=[("full", h)], epilogue=lambda acc, hh: hh + acc, name="mla_o")


def kernel(x_prompt, x_sample, state_wkv, state_shift, cache_latent, cache_kpe, page_table, p_prompt, p_sample, a_norm, a_mu, a_wr, a_wk, a_wv, a_wo, a_w0, a_w1, a_w2, a_a0, a_a1, a_a2, a_g1, a_g2, a_kk, a_ka, a_rk, a_lnx_w, a_lnx_b, kv_norm, kv_wdown, kv_latent_norm, kv_wuk, kv_wuv, b_norm, b_wdq, b_qnorm, b_wuq, b_wo, f_norm, f_wgroup, f_bgroup, f_wexpert, f_bexpert, f_wgate, f_wup, f_wdown, pl_wproj, pl_norm, pl_wgate, final_norm):
    nb, seq, D = x_prompt.shape
    ns, dec_seq, _ = x_sample.shape
    assert dec_seq == 1
    depth = f_norm.shape[0]
    n_a = a_norm.shape[0]
    n_prompt = nb * seq
    dc = kv_latent_norm.shape[0]
    dr = kv_wdown.shape[1] - dc
    past_len = page_table.shape[1] * cache_latent.shape[1]
    pos = jnp.concatenate([jnp.tile(jnp.arange(seq, dtype=I32), nb), jnp.full((ns,), past_len, I32)])

    h = jnp.concatenate([x_prompt.reshape(n_prompt, D), x_sample.reshape(ns, D)], 0)
    p_all = jnp.concatenate([p_prompt.reshape(depth, n_prompt, -1), p_sample.reshape(depth, ns, -1)], 1)
    wkv_p, wkv_s, sh_p, sh_s = [], [], [], []
    c_f32 = c_bf = None
    cache_kpe_t = jnp.swapaxes(cache_kpe, 1, 2)
    aw = (a_norm, a_mu, a_wr, a_wk, a_wv, a_wo, a_w0, a_w1, a_w2, a_a0, a_a1, a_a2, a_g1, a_g2,
          a_kk, a_ka, a_rk, a_lnx_w, a_lnx_b)
    for i in range(depth):
        if i < n_a:
            h, s_p, s_s, xn = _rwkv_layer(h, state_wkv[i], state_shift[i], n_prompt, nb, seq, aw, i)
            wkv_p.append(s_p)
            wkv_s.append(s_s)
            sh_p.append(xn[seq - 1:n_prompt:seq])
            sh_s.append(xn[n_prompt:])
        else:
            j = i - n_a
            bw = (b_norm[j], b_wdq, b_qnorm[j], b_wuq[j], kv_wuk, kv_wuv, b_wo)
            h = _mla_layer(h, pos, c_bf, n_prompt, nb, seq, cache_latent, cache_kpe_t, page_table, bw, j)
        h = _moe(h, f_norm[i], f_wgroup[i], f_bgroup[i], f_wexpert[i], f_bexpert[i],
                 f_wgate, f_wup, f_wdown, i)
        h = _ple_layer(h, p_all, pl_wproj, pl_norm[i], pl_wgate, i)
        if i == n_a - 1:
            c_f32, c_bf = _shared_kv(h, pos, kv_norm, kv_wdown, kv_latent_norm, dc, dr)
    y_p = _norm(h, final_norm, F32, 0, n_prompt)
    y_s = _norm(h, final_norm, F32, n_prompt, ns)
    lat, kpe = c_f32[:, :dc], c_f32[:, dc:dc + dr]
    return (y_p.reshape(nb, seq, D), y_s.reshape(ns, 1, D),
            jnp.stack(wkv_p), jnp.stack(sh_p),
            lat[:n_prompt].reshape(nb, seq, dc), kpe[:n_prompt].reshape(nb, seq, dr),
            jnp.stack(wkv_s), jnp.stack(sh_s),
            lat[n_prompt:].reshape(ns, 1, dc), kpe[n_prompt:].reshape(ns, 1, dr))
```

```python
import functools

import jax
import jax.numpy as jnp
from jax import lax
from jax.experimental import pallas as pl
from jax.experimental.pallas import tpu as pltpu

F32 = jnp.float32
BF16 = jnp.bfloat16
I32 = jnp.int32

RMS_EPS = 1e-6
GN_EPS = 64e-5
ROPE_THETA = 10000.0
RWKV_HEAD = 64
LANES = 128
SUBLANES = 8
STATE_LANES = 256
HEADS_PER_TILE = STATE_LANES // RWKV_HEAD
ROW_TILE = 128
EXPERT_TILE = 128
TOP_K = 2
VMEM_LIMIT = 56 * 1024 * 1024


def _cparams(*sem):
    return pltpu.CompilerParams(dimension_semantics=sem, vmem_limit_bytes=VMEM_LIMIT)


def _pick_tile(n, target, mult=16):
    best = None
    for t in range(mult, min(n, target) + 1, mult):
        if n % t == 0:
            best = t
    assert best is not None, (n, target, mult)
    return best


def _rms(x, g):
    return x * lax.rsqrt(jnp.mean(x * x, -1, keepdims=True) + RMS_EPS) * g


def _split_bf16(x):
    hi = x.astype(BF16)
    lo = (x - hi.astype(F32)).astype(BF16)
    return hi, lo


def _seg_ones(width=LANES):
    r = lax.broadcasted_iota(I32, (width, width), 0) // RWKV_HEAD
    c = lax.broadcasted_iota(I32, (width, width), 1) // RWKV_HEAD
    return (r == c).astype(BF16)


def _seg_sum(x, seg):
    outs = []
    for j in range(x.shape[1] // LANES):
        hi, lo = _split_bf16(x[:, j * LANES:(j + 1) * LANES])
        outs.append(jnp.dot(hi, seg, preferred_element_type=F32)
                    + jnp.dot(lo, seg, preferred_element_type=F32))
    return jnp.concatenate(outs, axis=1)


def _mm_kernel(*refs, n_extra, n_out, epilogue, normed, n_mm):
    x_refs = refs[0:2 * n_mm:2]
    w_refs = refs[1:2 * n_mm:2]
    extra_refs = refs[2 * n_mm:2 * n_mm + n_extra]
    out_refs = refs[2 * n_mm + n_extra:2 * n_mm + n_extra + n_out]
    wb_refs = refs[2 * n_mm + n_extra + n_out:]

    @pl.when(pl.program_id(1) == 0)
    def _():
        for w_ref, wb_ref in zip(w_refs, wb_refs):
            wb_ref[...] = w_ref[...].astype(BF16)

    x = x_refs[0][...]
    if normed:
        x = _rms(x, extra_refs[-1][...])
        extra_refs = extra_refs[:-1]
    accs = [jnp.dot(x.astype(BF16), wb_refs[0][...], preferred_element_type=F32)]
    for x_ref, wb_ref in zip(x_refs[1:], wb_refs[1:]):
        accs.append(jnp.dot(x_ref[...].astype(BF16), wb_ref[...], preferred_element_type=F32))
    outs = epilogue(*accs, *[r[...] for r in extra_refs]) if epilogue is not None else accs[0]
    if not isinstance(outs, (tuple, list)):
        outs = (outs,)
    for o_ref, o in zip(out_refs, outs):
        o_ref[...] = o.astype(o_ref.dtype)


def _mm(x, w, *, slab=None, layer=None, norm_gain=None, second=None, epilogue=None, extras=(),
        out_dtypes=(F32,), out_cols=None, tn=512, tm_target=832, name="mm"):
    M, K = x.shape[-2:]
    N = w.shape[-1]
    tn = min(tn, N)
    assert N % tn == 0
    tm = _pick_tile(M, tm_target)
    extras = list(extras)
    if norm_gain is not None:
        extras.append(("const", norm_gain.reshape(1, K)))
    out_cols = out_cols or [tn] * len(out_dtypes)
    def operand_specs(xa, wa, sl, ly):
        ka = xa.shape[-1]
        if xa.ndim == 3:
            xs = pl.BlockSpec((None, tm, ka), lambda j, i: (sl, i, 0))
        else:
            xs = pl.BlockSpec((tm, ka), lambda j, i: (i, 0))
        if wa.ndim == 3:
            ws = pl.BlockSpec((None, ka, tn), lambda j, i: (ly, 0, j))
        else:
            ws = pl.BlockSpec((ka, tn), lambda j, i: (0, j))
        return [xs, ws]

    in_specs = operand_specs(x, w, slab, layer)
    args = [x, w]
    scratch = [pltpu.VMEM((K, tn), BF16)]
    if second is not None:
        x2, w2, slab2, layer2 = second
        in_specs += operand_specs(x2, w2, slab2, layer2)
        args += [x2, w2]
        scratch.append(pltpu.VMEM((x2.shape[-1], tn), BF16))
    for kind, arr in extras:
        if kind == "row":
            in_specs.append(pl.BlockSpec((1, tn), lambda j, i: (0, j)))
        elif kind == "full":
            in_specs.append(pl.BlockSpec((tm, tn), lambda j, i: (i, j)))
        elif kind == "rows":
            in_specs.append(pl.BlockSpec((tm, arr.shape[1]), lambda j, i: (i, 0)))
        elif kind == "const":
            in_specs.append(pl.BlockSpec(arr.shape, lambda j, i: (0,) * arr.ndim))
        else:
            raise ValueError(kind)
        args.append(arr)
    nj = N // tn
    out_shape = [jax.ShapeDtypeStruct((M, oc * nj), dt) for dt, oc in zip(out_dtypes, out_cols)]
    out_specs = [pl.BlockSpec((tm, oc), lambda j, i: (i, j)) for oc in out_cols]
    res = pl.pallas_call(
        functools.partial(_mm_kernel, n_extra=len(extras), n_out=len(out_dtypes), epilogue=epilogue,
                          normed=norm_gain is not None, n_mm=len(scratch)),
        out_shape=out_shape,
        grid=(nj, M // tm),
        in_specs=in_specs,
        out_specs=out_specs,
        scratch_shapes=scratch,
        compiler_params=_cparams("parallel", "arbitrary"),
        name=name,
    )(*args)
    return res[0] if len(res) == 1 else res


def _norm_kernel(h_ref, g_ref, o_ref):
    o_ref[...] = _rms(h_ref[...], g_ref[...]).astype(o_ref.dtype)


def _norm(h, g, out_dtype, row0=0, nrows=None):
    D = h.shape[1]
    M = h.shape[0] - row0 if nrows is None else nrows
    assert row0 % ROW_TILE == 0 and M % ROW_TILE == 0
    base = row0 // ROW_TILE
    return pl.pallas_call(
        _norm_kernel,
        out_shape=jax.ShapeDtypeStruct((M, D), out_dtype),
        grid=(M // ROW_TILE,),
        in_specs=[pl.BlockSpec((ROW_TILE, D), lambda i: (base + i, 0)),
                  pl.BlockSpec((1, D), lambda i: (0, 0))],
        out_specs=pl.BlockSpec((ROW_TILE, D), lambda i: (i, 0)),
        compiler_params=_cparams("parallel"),
        name="rmsnorm",
    )(h, g.reshape(1, D))


def _mix_kernel(h_ref, g_ref, mu_ref, sp_ref, xm_ref, xn_ref, carry_ref, *, tiles_per_seq, n_prompt_tiles):
    i = pl.program_id(0)
    xn = _rms(h_ref[...], g_ref[...])
    xn_ref[...] = xn
    prev = jnp.where(i % tiles_per_seq == 0, 0.0, carry_ref[...])
    row = lax.broadcasted_iota(I32, xn.shape, 0)
    shifted = jnp.where(row == 0, prev, pltpu.roll(xn, 1, 0))
    shifted = jnp.where(i >= n_prompt_tiles, sp_ref[...], shifted)
    carry_ref[...] = xn[ROW_TILE - 1:ROW_TILE, :]
    dx = shifted - xn
    for j in range(mu_ref.shape[0]):
        xm_ref[j] = (xn + dx * mu_ref[j:j + 1, :]).astype(BF16)


def _norm_shift_mix(h, g, mu, state_shift, seq, n_prompt):
    M, D = h.shape
    assert seq % ROW_TILE == 0 and n_prompt % ROW_TILE == 0 and (M - n_prompt) % ROW_TILE == 0
    npt = n_prompt // ROW_TILE
    nmix = mu.shape[0]
    return pl.pallas_call(
        functools.partial(_mix_kernel, tiles_per_seq=seq // ROW_TILE, n_prompt_tiles=npt),
        out_shape=[jax.ShapeDtypeStruct((nmix, M, D), BF16), jax.ShapeDtypeStruct((M, D), F32)],
        grid=(M // ROW_TILE,),
        in_specs=[pl.BlockSpec((ROW_TILE, D), lambda i: (i, 0)),
                  pl.BlockSpec((1, D), lambda i: (0, 0)),
                  pl.BlockSpec((nmix, D), lambda i: (0, 0)),
                  pl.BlockSpec((ROW_TILE, D), lambda i: (jnp.maximum(i - npt, 0), 0))],
        out_specs=[pl.BlockSpec((nmix, ROW_TILE, D), lambda i: (0, i, 0)),
                   pl.BlockSpec((ROW_TILE, D), lambda i: (i, 0))],
        scratch_shapes=[pltpu.VMEM((1, D), F32)],
        compiler_params=_cparams("arbitrary"),
        name="norm_shift_mix",
    )(h, g.reshape(1, D), mu, state_shift)


def _wkv_post_kernel(*refs):
    y_refs = refs[:HEADS_PER_TILE]
    r_ref, k_ref, v_ref, g_ref, lw_ref, lb_ref, rk_ref, o_ref = refs[HEADS_PER_TILE:]
    seg = _seg_ones()
    inv = 1.0 / RWKV_HEAD
    y = _heads_from_parts([y_ref[...] for y_ref in y_refs], r_ref.shape[1] // RWKV_HEAD)
    mean = _seg_sum(y, seg) * inv
    d = y - mean
    var = _seg_sum(d * d, seg) * inv
    yn = d * lax.rsqrt(var + GN_EPS) * lw_ref[...] + lb_ref[...]
    v = v_ref[...]
    bonus = _seg_sum(r_ref[...] * k_ref[...] * rk_ref[...], seg) * v
    o_ref[...] = ((yn + bonus) * g_ref[...]).astype(o_ref.dtype)


def _wkv_post(y_parts, r, k2, v, g, lnx_w, lnx_b, r_k):
    M, D = r.shape
    big = pl.BlockSpec((ROW_TILE, D), lambda i: (i, 0))
    part = pl.BlockSpec((ROW_TILE, D // HEADS_PER_TILE), lambda i: (i, 0))
    row = pl.BlockSpec((1, D), lambda i: (0, 0))
    return pl.pallas_call(
        _wkv_post_kernel,
        out_shape=jax.ShapeDtypeStruct((M, D), BF16),
        grid=(M // ROW_TILE,),
        in_specs=[part] * HEADS_PER_TILE + [big, big, big, big, row, row, row],
        out_specs=big,
        compiler_params=_cparams("parallel"),
        name="wkv_post",
    )(*y_parts, r, k2, v, g, lnx_w.reshape(1, D), lnx_b.reshape(1, D), r_k.reshape(1, D))


def _wkv_consts(ntile):
    seg = _seg_ones(STATE_LANES)
    rows = ntile * RWKV_HEAD
    rr = lax.broadcasted_iota(I32, (rows, STATE_LANES), 0) % RWKV_HEAD
    cc = lax.broadcasted_iota(I32, (rows, STATE_LANES), 1) % RWKV_HEAD
    diag = rr == cc
    eh = (lax.broadcasted_iota(I32, (SUBLANES, STATE_LANES), 0)
          == lax.broadcasted_iota(I32, (SUBLANES, STATE_LANES), 1) // RWKV_HEAD).astype(BF16)
    return seg, diag, eh


def _bcast_row(x, i, ntile):
    return jnp.concatenate(
        [jnp.broadcast_to(x[i:i + 1, j * STATE_LANES:(j + 1) * STATE_LANES], (RWKV_HEAD, STATE_LANES))
         for j in range(ntile)], axis=0)


def _wkv_step(S, vals, i, ntile, consts):
    r8, w8, k8, v8, a8, b8 = vals
    seg, diag, eh = consts
    sa = jnp.dot((S * _bcast_row(a8, i, ntile)).astype(BF16), seg, preferred_element_type=F32)
    vb = jnp.dot(jnp.where(diag, _bcast_row(v8, i, ntile), 0.0).astype(BF16), seg, preferred_element_type=F32)
    s_new = S * _bcast_row(w8, i, ntile) + sa * _bcast_row(b8, i, ntile) + vb * _bcast_row(k8, i, ntile)
    p = (s_new * _bcast_row(r8, i, ntile)).astype(BF16)
    y_t = lax.dot_general(eh, p, (((1,), (1,)), ((), ())), preferred_element_type=F32)
    return s_new, y_t


def _wkv_seq_kernel(*refs, nb, tc, ntile):
    in_refs = refs[:6 * nb]
    s0_ref = refs[6 * nb]
    y_refs = refs[6 * nb + 1:6 * nb + 1 + HEADS_PER_TILE]
    s_ref = refs[6 * nb + 1 + HEADS_PER_TILE]
    consts = _wkv_consts(ntile)

    @pl.when(pl.program_id(1) == 0)
    def _():
        s_ref[...] = s0_ref[...]

    def body(t8, carry):
        base = pl.multiple_of(t8 * SUBLANES, SUBLANES)
        vals = [[ref[pl.ds(base, SUBLANES), :] for ref in in_refs[6 * u:6 * u + 6]] for u in range(nb)]
        S = [s_ref[u] for u in range(nb)]
        ys = [[] for _ in range(nb)]
        for i in range(SUBLANES):
            for u in range(nb):
                S[u], y_t = _wkv_step(S[u], vals[u], i, ntile, consts)
                ys[u].append(y_t)
        for u in range(nb):
            s_ref[u] = S[u]
            for hs in range(HEADS_PER_TILE):
                y_refs[hs][u, pl.ds(base, SUBLANES), :] = jnp.concatenate(
                    [y[hs:hs + 1, :] for y in ys[u]], axis=0)
        return carry

    lax.fori_loop(0, tc // SUBLANES, body, 0)


def _wkv_scan_seq(rows, s0, *, nseq, seq, row0, nb, tc):
    D = rows[0].shape[1]
    ntile = D // STATE_LANES
    assert nseq % nb == 0 and seq % tc == 0 and row0 % tc == 0 and tc % SUBLANES == 0
    nt = seq // tc
    base = row0 // tc
    in_specs, args = [], []
    for u in range(nb):
        for arr in rows:
            in_specs.append(pl.BlockSpec((tc, D), lambda g, c, u=u: (base + (g * nb + u) * nt + c, 0)))
            args.append(arr)
    srows = ntile * RWKV_HEAD
    in_specs.append(pl.BlockSpec((nb, srows, STATE_LANES), lambda g, c: (g, 0, 0)))
    args.append(s0)
    half = D // HEADS_PER_TILE
    y_spec = pl.BlockSpec((nb, tc, half), lambda g, c: (g, c, 0))
    res = pl.pallas_call(
        functools.partial(_wkv_seq_kernel, nb=nb, tc=tc, ntile=ntile),
        out_shape=[jax.ShapeDtypeStruct((nseq, seq, half), F32)] * HEADS_PER_TILE
        + [jax.ShapeDtypeStruct((nseq, srows, STATE_LANES), F32)],
        grid=(nseq // nb, nt),
        in_specs=in_specs,
        out_specs=[y_spec] * HEADS_PER_TILE + [pl.BlockSpec((nb, srows, STATE_LANES), lambda g, c: (g, 0, 0))],
        compiler_params=_cparams("parallel", "arbitrary"),
        name="wkv_scan_seq",
    )(*args)
    return res[:HEADS_PER_TILE], res[HEADS_PER_TILE]


def _wkv_one_kernel(*refs, nu, ntile):
    in_refs = refs[:6]
    s0_ref = refs[6]
    y_refs = refs[7:7 + HEADS_PER_TILE]
    s_ref = refs[7 + HEADS_PER_TILE]
    consts = _wkv_consts(ntile)
    vals = [ref[...] for ref in in_refs]
    ys = []
    for u in range(nu):
        s_new, y_t = _wkv_step(s0_ref[u], vals, u, ntile, consts)
        s_ref[u] = s_new
        ys.append(y_t)
    for hs in range(HEADS_PER_TILE):
        y_refs[hs][...] = jnp.concatenate([y[hs:hs + 1, :] for y in ys], axis=0)


def _wkv_scan_one(rows, s0, *, nseq, row0, nu=SUBLANES):
    D = rows[0].shape[1]
    ntile = D // STATE_LANES
    assert nu == SUBLANES and nseq % nu == 0 and row0 % nu == 0
    base = row0 // nu
    srows = ntile * RWKV_HEAD
    half = D // HEADS_PER_TILE
    in_specs = [pl.BlockSpec((nu, D), lambda g: (base + g, 0)) for _ in rows]
    in_specs.append(pl.BlockSpec((nu, srows, STATE_LANES), lambda g: (g, 0, 0)))
    y_spec = pl.BlockSpec((nu, half), lambda g: (g, 0))
    res = pl.pallas_call(
        functools.partial(_wkv_one_kernel, nu=nu, ntile=ntile),
        out_shape=[jax.ShapeDtypeStruct((nseq, half), F32)] * HEADS_PER_TILE
        + [jax.ShapeDtypeStruct((nseq, srows, STATE_LANES), F32)],
        grid=(nseq // nu,),
        in_specs=in_specs,
        out_specs=[y_spec] * HEADS_PER_TILE + [pl.BlockSpec((nu, srows, STATE_LANES), lambda g: (g, 0, 0))],
        compiler_params=_cparams("parallel"),
        name="wkv_scan_one",
    )(*rows, s0)
    return res[:HEADS_PER_TILE], res[HEADS_PER_TILE]


def _state_to_tiles(s):
    n, H = s.shape[0], s.shape[1]
    s = s.reshape(n, H // HEADS_PER_TILE, HEADS_PER_TILE, RWKV_HEAD, RWKV_HEAD)
    return jnp.transpose(s, (0, 1, 3, 2, 4)).reshape(n, H // HEADS_PER_TILE * RWKV_HEAD, STATE_LANES)


def _tiles_to_state(s, H):
    n = s.shape[0]
    s = s.reshape(n, H // HEADS_PER_TILE, RWKV_HEAD, HEADS_PER_TILE, RWKV_HEAD)
    return jnp.transpose(s, (0, 1, 3, 2, 4)).reshape(n, H, RWKV_HEAD, RWKV_HEAD)


def _heads_from_parts(parts, H):
    return jnp.concatenate(
        [parts[h % HEADS_PER_TILE][:, (h // HEADS_PER_TILE) * RWKV_HEAD:(h // HEADS_PER_TILE + 1) * RWKV_HEAD]
         for h in range(H)], axis=1)


def _router_kernel(h_ref, g_ref, w_ref, b_ref, gid_ref, xb_ref, info_ref, wh_ref, wl_ref, *, n_groups):
    @pl.when(pl.program_id(0) == 0)
    def _():
        wh, wl = _split_bf16(w_ref[...])
        wh_ref[...] = wh
        wl_ref[...] = wl

    xn = _rms(h_ref[...], g_ref[...])
    xh, xl = _split_bf16(xn)
    xb_ref[...] = xn
    logits = (jnp.dot(xh, wh_ref[...], preferred_element_type=F32)
              + jnp.dot(xl, wh_ref[...], preferred_element_type=F32)
              + jnp.dot(xh, wl_ref[...], preferred_element_type=F32)) + b_ref[...]
    lane = lax.broadcasted_iota(I32, logits.shape, 1)
    lanef = lane.astype(F32)
    neg = -jnp.inf
    big = float(LANES)
    lg = jnp.where(lane < n_groups, logits, neg)
    mg = jnp.max(lg, -1, keepdims=True)
    g_top = jnp.min(jnp.where(lg == mg, lanef, big), -1, keepdims=True)
    p_sel = 1.0 / jnp.sum(jnp.exp(lg - mg), -1, keepdims=True)
    le = jnp.where(gid_ref[...] == g_top, logits, neg)
    v1 = jnp.max(le, -1, keepdims=True)
    i1 = jnp.min(jnp.where(le == v1, lanef, big), -1, keepdims=True)
    le2 = jnp.where(lanef == i1, neg, le)
    v2 = jnp.max(le2, -1, keepdims=True)
    i2 = jnp.min(jnp.where(le2 == v2, lanef, big), -1, keepdims=True)
    e2 = jnp.exp(v2 - v1)
    den = 1.0 + e2
    gate1 = (1.0 / den) * p_sel
    gate2 = (e2 / den) * p_sel
    info = jnp.where(lane == 0, i1 - n_groups,
                     jnp.where(lane == 1, i2 - n_groups,
                               jnp.where(lane == 2, gate1, jnp.where(lane == 3, gate2, 0.0))))
    info_ref[...] = info


def _router(h, g, w_route, b_route, n_groups, per_group):
    M, D = h.shape
    lane = jnp.arange(LANES, dtype=I32)
    is_expert = (lane >= n_groups) & (lane < n_groups * (1 + per_group))
    gid = jnp.where(is_expert, (lane - n_groups) // per_group, -1).astype(F32).reshape(1, LANES)
    return pl.pallas_call(
        functools.partial(_router_kernel, n_groups=n_groups),
        out_shape=[jax.ShapeDtypeStruct((M, D), F32), jax.ShapeDtypeStruct((M, LANES), F32)],
        grid=(M // ROW_TILE,),
        in_specs=[pl.BlockSpec((ROW_TILE, D), lambda i: (i, 0)),
                  pl.BlockSpec((1, D), lambda i: (0, 0)),
                  pl.BlockSpec((D, LANES), lambda i: (0, 0)),
                  pl.BlockSpec((1, LANES), lambda i: (0, 0)),
                  pl.BlockSpec((1, LANES), lambda i: (0, 0))],
        out_specs=[pl.BlockSpec((ROW_TILE, D), lambda i: (i, 0)),
                   pl.BlockSpec((ROW_TILE, LANES), lambda i: (i, 0))],
        scratch_shapes=[pltpu.VMEM((D, LANES), BF16), pltpu.VMEM((D, LANES), BF16)],
        compiler_params=_cparams("arbitrary"),
        name="moe_router",
    )(h, g.reshape(1, D), w_route, b_route, gid)


def _expert_kernel(te_ref, used_ref, tok_ref, x_hbm, gate_ref, wg_ref, wu_ref, wd_ref, o_ref,
                   xbuf, sem, wgb, wub, wdb):
    i = pl.program_id(0)
    used = used_ref[0]
    slot = i % 2
    new_expert = jnp.logical_or(i == 0, te_ref[i] != te_ref[jnp.maximum(i - 1, 0)])

    def row_copy(tile, r, sl):
        tok = tok_ref[tile * EXPERT_TILE + r]
        return pltpu.make_async_copy(x_hbm.at[pl.ds(tok, 1), :], xbuf.at[sl, pl.ds(r, 1), :], sem.at[sl])

    def start_tile(tile, sl):
        for r in range(EXPERT_TILE):
            row_copy(tile, r, sl).start()

    @pl.when(jnp.logical_and(i == 0, used > 0))
    def _():
        start_tile(0, 0)

    @pl.when(i + 1 < used)
    def _():
        start_tile(i + 1, 1 - slot)

    @pl.when(jnp.logical_and(i < used, new_expert))
    def _():
        wgb[...] = wg_ref[0].astype(BF16)
        wub[...] = wu_ref[0].astype(BF16)
        wdb[...] = wd_ref[0].astype(BF16)

    @pl.when(i < used)
    def _():
        for r in range(EXPERT_TILE):
            row_copy(i, r, slot).wait()
        x = xbuf[slot].astype(BF16)
        hg = jnp.dot(x, wgb[...], preferred_element_type=F32)
        hu = jnp.dot(x, wub[...], preferred_element_type=F32)
        hid = (hg * jax.nn.sigmoid(hg)) * hu * gate_ref[...]
        o_ref[...] = jnp.dot(hid.astype(BF16), wdb[...], preferred_element_type=F32)

    @pl.when(i >= used)
    def _():
        o_ref[...] = jnp.zeros_like(o_ref)


def _experts(tile_expert, used, row_token, x, row_gate, w_gate, w_up, w_down, layer):
    R = row_token.shape[0]
    D = x.shape[1]
    _, E, _, Fd = w_gate.shape
    nt = R // EXPERT_TILE
    return pl.pallas_call(
        _expert_kernel,
        out_shape=jax.ShapeDtypeStruct((R, D), F32),
        grid_spec=pltpu.PrefetchScalarGridSpec(
            num_scalar_prefetch=3,
            grid=(nt,),
            in_specs=[pl.BlockSpec(memory_space=pl.ANY),
                      pl.BlockSpec((EXPERT_TILE, 1), lambda i, te, u, tk: (i, 0)),
                      pl.BlockSpec((None, 1, D, Fd), lambda i, te, u, tk: (layer, te[i], 0, 0)),
                      pl.BlockSpec((None, 1, D, Fd), lambda i, te, u, tk: (layer, te[i], 0, 0)),
                      pl.BlockSpec((None, 1, Fd, D), lambda i, te, u, tk: (layer, te[i], 0, 0))],
            out_specs=pl.BlockSpec((EXPERT_TILE, D), lambda i, te, u, tk: (i, 0)),
            scratch_shapes=[pltpu.VMEM((2, EXPERT_TILE, D), F32), pltpu.SemaphoreType.DMA((2,)),
                            pltpu.VMEM((D, Fd), BF16), pltpu.VMEM((D, Fd), BF16), pltpu.VMEM((Fd, D), BF16)],
        ),
        compiler_params=_cparams("arbitrary"),
        name="moe_experts",
    )(tile_expert, used, row_token, x, row_gate, w_gate, w_up, w_down)


def _moe(h, f_norm, w_group, b_group, w_expert, b_expert, w_gate, w_up, w_down, layer):
    M, D = h.shape
    G = w_group.shape[1]
    P = w_expert.shape[2]
    E = G * P
    assert G + E <= LANES
    w_route = jnp.zeros((D, LANES), F32)
    w_route = w_route.at[:, :G].set(w_group)
    w_route = w_route.at[:, G:G + E].set(jnp.transpose(w_expert, (1, 0, 2)).reshape(D, E))
    b_route = jnp.zeros((1, LANES), F32)
    b_route = b_route.at[0, :G].set(b_group)
    b_route = b_route.at[0, G:G + E].set(b_expert.reshape(E))
    xn, info = _router(h, f_norm, w_route, b_route, G, P)

    eid = info[:, :TOP_K].astype(I32).reshape(-1)
    gates = info[:, TOP_K:2 * TOP_K].reshape(-1)
    npair = M * TOP_K
    nt = -(-npair // EXPERT_TILE) + E
    R = nt * EXPERT_TILE
    onehot = (eid[:, None] == jnp.arange(E, dtype=I32)[None, :]).astype(I32)
    csum = jnp.cumsum(onehot, axis=0)
    counts = csum[-1]
    rank = jnp.sum(csum * onehot, axis=1) - 1
    padded = (counts + EXPERT_TILE - 1) // EXPERT_TILE * EXPERT_TILE
    pend = jnp.cumsum(padded)
    pstart = pend - padded
    dest = jnp.sum(onehot * pstart[None, :], axis=1) + rank
    token = (jnp.arange(npair, dtype=I32) // TOP_K).astype(F32)
    rows = jnp.zeros((R, 2), F32).at[dest].set(jnp.stack([token, gates], -1))
    row_token = rows[:, 0].astype(I32)
    tile_row0 = jnp.arange(nt, dtype=I32) * EXPERT_TILE
    tile_expert = jnp.minimum(jnp.sum((pend[None, :] <= tile_row0[:, None]).astype(I32), axis=1), E - 1)
    used = (pend[-1] // EXPERT_TILE).astype(I32).reshape(1)

    ys = _experts(tile_expert, used, row_token, xn, rows[:, 1:2], w_gate, w_up, w_down, layer)
    dest = dest.reshape(M, TOP_K)
    y = jnp.take(ys, dest[:, 0], axis=0)
    for s in range(1, TOP_K):
        y = y + jnp.take(ys, dest[:, s], axis=0)
    return h + y


def _row_reduce(x, combine, reduce):
    t = x[:, 0:LANES]
    for j in range(1, x.shape[1] // LANES):
        t = combine(t, x[:, j * LANES:(j + 1) * LANES])
    return reduce(t, -1, keepdims=True)


def _attn_prompt_kernel(qn_ref, qp_ref, kv_ref, wuk_ref, wuv_ref, o_ref, q_s, m_s, l_s, acc_s,
                        *, tq, tk, nh, dn, dc, scale):
    qi = pl.program_id(1)
    kv = pl.program_id(2)
    last = (qi * tq + tq - 1) // tk
    rows = nh * tq

    @pl.when(kv == 0)
    def _():
        for h in range(nh):
            qa = jnp.dot(qn_ref[:, h * dn:(h + 1) * dn], wuk_ref[h], preferred_element_type=F32) * scale
            q_s[h * tq:(h + 1) * tq, 0:dc] = qa.astype(BF16)
            q_s[h * tq:(h + 1) * tq, dc:] = qp_ref[:, h * LANES:(h + 1) * LANES]
        m_s[...] = jnp.full(m_s.shape, -jnp.inf, F32)
        l_s[...] = jnp.zeros(l_s.shape, F32)
        acc_s[...] = jnp.zeros(acc_s.shape, F32)

    def update(masked):
        kblk = kv_ref[...]
        s = lax.dot_general(q_s[...], kblk, (((1,), (1,)), ((), ())), preferred_element_type=F32)
        if masked:
            qpos = qi * tq + lax.broadcasted_iota(I32, (rows, tk), 0) % tq
            kpos = kv * tk + lax.broadcasted_iota(I32, (rows, tk), 1)
            s = jnp.where(kpos <= qpos, s, -jnp.inf)
        m_old = m_s[...]
        m_new = jnp.maximum(m_old, _row_reduce(s, jnp.maximum, jnp.max))
        alpha = jnp.exp(m_old - m_new)
        p = jnp.exp(s - m_new)
        l_s[...] = alpha * l_s[...] + _row_reduce(p, jnp.add, jnp.sum)
        acc_s[...] = alpha * acc_s[...] + jnp.dot(p.astype(BF16), kblk[:, 0:dc], preferred_element_type=F32)
        m_s[...] = m_new

    @pl.when(kv < last)
    def _():
        update(False)

    @pl.when(kv == last)
    def _():
        update(True)
        o = acc_s[...] / l_s[...]
        for h in range(nh):
            oh = jnp.dot(o[h * tq:(h + 1) * tq].astype(BF16), wuv_ref[h], preferred_element_type=F32)
            o_ref[:, h * oh.shape[1]:(h + 1) * oh.shape[1]] = oh.astype(o_ref.dtype)


def _attn_prompt(qn, qp, kvb, wuk_t, wuv_t, *, nb, seq, tq, tk, scale):
    nh, dn, dc = wuk_t.shape
    dv = wuv_t.shape[2]
    width = kvb.shape[1]
    assert width == dc + LANES and qp.shape[1] == nh * LANES
    nq, nk = seq // tq, seq // tk

    def kv_map(b, qi, kv):
        return (b * nk + jnp.minimum(kv, (qi * tq + tq - 1) // tk), 0)

    return pl.pallas_call(
        functools.partial(_attn_prompt_kernel, tq=tq, tk=tk, nh=nh, dn=dn, dc=dc, scale=scale),
        out_shape=jax.ShapeDtypeStruct((nb * seq, nh * dv), BF16),
        grid=(nb, nq, nk),
        in_specs=[pl.BlockSpec((tq, nh * dn), lambda b, qi, kv: (b * nq + qi, 0)),
                  pl.BlockSpec((tq, nh * LANES), lambda b, qi, kv: (b * nq + qi, 0)),
                  pl.BlockSpec((tk, width), kv_map),
                  pl.BlockSpec((nh, dn, dc), lambda b, qi, kv: (0, 0, 0)),
                  pl.BlockSpec((nh, dc, dv), lambda b, qi, kv: (0, 0, 0))],
        out_specs=pl.BlockSpec((tq, nh * dv), lambda b, qi, kv: (b * nq + qi, 0)),
        scratch_shapes=[pltpu.VMEM((nh * tq, width), BF16),
                        pltpu.VMEM((nh * tq, 1), F32),
                        pltpu.VMEM((nh * tq, 1), F32),
                        pltpu.VMEM((nh * tq, dc), F32)],
        compiler_params=_cparams("parallel", "parallel", "arbitrary"),
        name="mla_prompt_attn",
    )(qn, qp, kvb, wuk_t, wuv_t)


def _attn_decode_kernel(pt_ref, *refs, npg, dr, dc):
    qa_ref, qr_ref, kvn_ref = refs[:3]
    lat_hbm, kpe_hbm, o_ref, m_s, l_s, acc_s, lat_s, kpe_s, latbuf, kpebuf, sem = refs[3:]
    g = pl.program_id(1)
    ng = pl.num_programs(1)
    t = pl.program_id(0) * ng + g
    total = pl.num_programs(0) * ng
    slot = t % 2

    def copies(step, sl):
        out = []
        for i in range(npg):
            pg = pt_ref[step * npg + i]
            out.append(pltpu.make_async_copy(lat_hbm.at[pg], latbuf.at[sl, i], sem.at[0, sl]))
            out.append(pltpu.make_async_copy(kpe_hbm.at[pg], kpebuf.at[sl, i], sem.at[1, sl]))
        return out

    @pl.when(t == 0)
    def _():
        for c in copies(0, 0):
            c.start()

    nxt = jnp.minimum(t + 1, total - 1)
    for c in copies(nxt, 1 - slot):
        c.start()
    for c in copies(t, slot):
        c.wait()

    @pl.when(g == 0)
    def _():
        m_s[...] = jnp.full(m_s.shape, -jnp.inf, F32)
        l_s[...] = jnp.zeros(l_s.shape, F32)
        acc_s[...] = jnp.zeros(acc_s.shape, F32)

    qa = qa_ref[0]
    qr = qr_ref[0][:, 0:dr]
    nt = (((1,), (1,)), ((), ()))
    page = latbuf.shape[2]
    for i in range(npg):
        lat_s[i * page:(i + 1) * page, :] = latbuf[slot, i].astype(BF16)
        kpe_s[:, i * page:(i + 1) * page] = kpebuf[slot, i].astype(BF16)
    lat = lat_s[...]
    s = (lax.dot_general(qa, lat, nt, preferred_element_type=F32)
         + jnp.dot(qr, kpe_s[...], preferred_element_type=F32))
    m_old = m_s[...]
    m_new = jnp.maximum(m_old, jnp.max(s, -1, keepdims=True))
    alpha = jnp.exp(m_old - m_new)
    p = jnp.exp(s - m_new)
    l_s[...] = alpha * l_s[...] + jnp.sum(p, -1, keepdims=True)
    acc_s[...] = alpha * acc_s[...] + jnp.dot(p.astype(BF16), lat, preferred_element_type=F32)
    m_s[...] = m_new

    @pl.when(g == pl.num_programs(1) - 1)
    def _():
        kvn = kvn_ref[0].astype(F32)
        s = (jnp.sum(qa.astype(F32) * kvn[:, 0:dc], -1, keepdims=True)
             + jnp.sum(qr.astype(F32) * kvn[:, dc:dc + dr], -1, keepdims=True))
        m_old = m_s[...]
        m_new = jnp.maximum(m_old, s)
        alpha = jnp.exp(m_old - m_new)
        p = jnp.exp(s - m_new)
        l = alpha * l_s[...] + p
        acc = alpha * acc_s[...] + p * kvn[:, 0:dc]
        o_ref[0] = (acc / l).astype(o_ref.dtype)

    @pl.when(t == total - 1)
    def _():
        for c in copies(nxt, 1 - slot):
            c.wait()


def _attn_decode(qa, qr, kvn, cache_latent, cache_kpe_t, page_table, *, dr, npg):
    nseq, npages = page_table.shape
    _, nh, dc = qa.shape
    width = kvn.shape[2]
    page = cache_latent.shape[1]
    assert npages % npg == 0

    def seq_spec(a, b):
        return pl.BlockSpec((1, a, b), lambda s, g, pt: (s, 0, 0))

    hbm = pl.BlockSpec(memory_space=pl.ANY)
    in_specs = [seq_spec(nh, dc), seq_spec(nh, LANES), seq_spec(1, width), hbm, hbm]
    return pl.pallas_call(
        functools.partial(_attn_decode_kernel, npg=npg, dr=dr, dc=dc),
        out_shape=jax.ShapeDtypeStruct((nseq, nh, dc), BF16),
        grid_spec=pltpu.PrefetchScalarGridSpec(
            num_scalar_prefetch=1,
            grid=(nseq, npages // npg),
            in_specs=in_specs,
            out_specs=seq_spec(nh, dc),
            scratch_shapes=[pltpu.VMEM((nh, 1), F32), pltpu.VMEM((nh, 1), F32), pltpu.VMEM((nh, dc), F32),
                            pltpu.VMEM((npg * page, dc), BF16), pltpu.VMEM((dr, npg * page), BF16),
                            pltpu.VMEM((2, npg, page, dc), F32), pltpu.VMEM((2, npg, dr, page), F32),
                            pltpu.SemaphoreType.DMA((2, 2))],
        ),
        compiler_params=_cparams("arbitrary", "arbitrary"),
        name="mla_decode_attn",
    )(page_table.reshape(-1), qa, qr, kvn, cache_latent, cache_kpe_t)


def _head_mm_kernel(x_ref, w_ref, o_ref, *, scale):
    acc = jnp.dot(x_ref[...], w_ref[0], preferred_element_type=F32)
    if scale is not None:
        acc = acc * scale
    o_ref[...] = acc.astype(o_ref.dtype)


def _head_mm(x, w, scale=None):
    R = x.shape[0]
    nh, kin, kout = w.shape
    return pl.pallas_call(
        functools.partial(_head_mm_kernel, scale=scale),
        out_shape=jax.ShapeDtypeStruct((R, nh * kout), BF16),
        grid=(nh,),
        in_specs=[pl.BlockSpec((R, kin), lambda h: (0, h)),
                  pl.BlockSpec((1, kin, kout), lambda h: (h, 0, 0))],
        out_specs=pl.BlockSpec((R, kout), lambda h: (0, h)),
        compiler_params=_cparams("parallel"),
        name="head_mm",
    )(x, w)


def _rope_tables(pos, dr):
    half = dr // 2
    inv = ROPE_THETA ** (-2.0 * jnp.arange(half, dtype=F32) / dr)
    ang = pos.astype(F32)[:, None] * inv[None, :]
    cos, sin = jnp.cos(ang), jnp.sin(ang)
    rep = LANES // dr
    cos_t = jnp.tile(jnp.concatenate([cos, cos], -1), (1, rep))
    sin_t = jnp.tile(jnp.concatenate([-sin, sin], -1), (1, rep))
    return cos_t, sin_t


def _swap_halves(w, dr):
    return jnp.concatenate([w[..., dr // 2:], w[..., :dr // 2]], -1)


def _rwkv_layer(h, state_wkv, state_shift, n_prompt, nb, seq, aw, li):
    (a_norm, mu, w_r, w_k, w_v, w_o, w0, w1, w2, a0, a1, a2, g1, g2, k_k, k_a, r_k, lnx_w, lnx_b) = aw
    M, D = h.shape
    H = D // RWKV_HEAD
    nsample = M - n_prompt
    row = lambda x: ("row", x[li].reshape(1, D))
    xm, xn = _norm_shift_mix(h, a_norm[li], mu[li], state_shift, seq, n_prompt)
    jr, jw, jk, jv, ja, jg = range(6)

    r = _mm(xm, w_r, slab=jr, layer=li, name="rwkv_r")
    v = _mm(xm, w_v, slab=jv, layer=li, name="rwkv_v")
    th = _mm(xm, w1, slab=jw, layer=li, epilogue=jnp.tanh, out_dtypes=(BF16,), name="rwkv_w1")

    def decay_epilogue(z, w0r):
        u = -(w0r + z)
        softplus = jnp.maximum(u, 0.0) + jnp.log(1.0 + jnp.exp(-jnp.abs(u)))
        return jnp.exp(-jnp.exp(-softplus - 0.5))

    decay = _mm(th, w2, layer=li, extras=[row(w0)], epilogue=decay_epilogue, name="rwkv_w2")
    al = _mm(xm, a1, slab=ja, layer=li, out_dtypes=(BF16,), name="rwkv_a1")
    a_lr = _mm(al, a2, layer=li, extras=[row(a0)], name="rwkv_a2",
               epilogue=lambda z, a0r: jax.nn.sigmoid(a0r + z))
    gl = _mm(xm, g1, slab=jg, layer=li, epilogue=jax.nn.sigmoid, out_dtypes=(BF16,), name="rwkv_g1")
    g = _mm(gl, g2, layer=li, name="rwkv_g2")

    def k_epilogue(k, a, kkr, kar):
        seg = _seg_ones()
        kk = k * kkr
        nrm = jnp.sqrt(_seg_sum(kk * kk, seg))
        kk = kk / jnp.maximum(nrm, 1e-12)
        return k * (1.0 + (a - 1.0) * kar), -kk, kk * a

    k2, a_neg, b_pos = _mm(xm, w_k, slab=jk, layer=li, extras=[("full", a_lr), row(k_k), row(k_a)],
                           epilogue=k_epilogue, out_dtypes=(F32, F32, F32), name="rwkv_k")

    rows = (r, decay, k2, v, a_neg, b_pos)
    s0_p = jnp.zeros((nb, H // HEADS_PER_TILE * RWKV_HEAD, STATE_LANES), F32)
    yp_p, s_p = _wkv_scan_seq(rows, s0_p, nseq=nb, seq=seq, row0=0, nb=min(nb, 4), tc=min(seq, 32))
    yp_s, s_s = _wkv_scan_one(rows, _state_to_tiles(state_wkv), nseq=nsample, row0=n_prompt)
    y_parts = [jnp.concatenate([a.reshape(n_prompt, -1), b], 0) for a, b in zip(yp_p, yp_s)]

    yo = _wkv_post(y_parts, r, k2, v, g, lnx_w[li], lnx_b[li], r_k[li].reshape(D))
    h = _mm(yo, w_o, layer=li, extras=[("full", h)], epilogue=lambda acc, hh: hh + acc, name="rwkv_o")
    return h, _tiles_to_state(s_p, H), _tiles_to_state(s_s, H), xn


def _ple_layer(h, p_all, w_proj, g_norm, w_gate, li):
    return _mm(h, w_gate, layer=li, norm_gain=g_norm, second=(p_all, w_proj, li, li), extras=[("full", h)],
               epilogue=lambda acc, pp, hh: hh + pp * jax.nn.sigmoid(acc), name="ple_gate")


def _shared_kv(h, pos, g_in, w_down, g_latent, dc, dr):
    M, D = h.shape
    cos_t, sin_t = _rope_tables(pos, dr)
    w_ext = jnp.concatenate([w_down, _swap_halves(w_down[:, dc:dc + dr], dr)], 1)

    assert 2 * dr == LANES

    def epilogue(acc, gl, ct, st):
        lat = _rms(acc[:, :dc], gl)
        t = acc[:, dc:]
        kpe = t * ct + pltpu.roll(t, dr, 1) * st
        out = jnp.concatenate([lat, kpe], 1)
        return out, out

    return _mm(h, w_ext, norm_gain=g_in,
               extras=[("const", g_latent.reshape(1, dc)), ("rows", cos_t), ("rows", sin_t)],
               epilogue=epilogue, out_dtypes=(F32, BF16), tn=dc + 2 * dr, name="kv_down")


def _mla_layer(h, pos, c_bf, n_prompt, nb, seq, cache_latent, cache_kpe_t, page_table, bw, li):
    b_norm, w_dq, g_q, w_uq, w_uk, w_uv, w_o = bw
    M, D = h.shape
    dc, nh, dn = w_uk.shape
    dv = w_uv.shape[2]
    ql = w_uq.shape[0]
    dr = w_uq.shape[1] // nh - dn
    scale = float(dn + dr) ** -0.5
    w_uq3 = w_uq.reshape(ql, nh, dn + dr)
    w_qn = w_uq3[:, :, :dn].reshape(ql, nh * dn)
    w_qr = w_uq3[:, :, dn:]
    assert 2 * dr == LANES
    w_qr_ext = jnp.concatenate([w_qr, _swap_halves(w_qr, dr)], -1).reshape(ql, nh * LANES)
    wuk_t = jnp.transpose(w_uk, (1, 2, 0)).astype(BF16)
    wuv_t = jnp.transpose(w_uv, (1, 0, 2)).astype(BF16)
    cos_t, sin_t = _rope_tables(pos, dr)

    cq = _mm(h, w_dq, layer=li, norm_gain=b_norm, extras=[("const", g_q.reshape(1, ql))], epilogue=_rms,
             out_dtypes=(BF16,), tn=ql, name="mla_dq")
    qn = _mm(cq, w_qn, out_dtypes=(BF16,), name="mla_uq_nope")

    def rope_epilogue(acc, ct, st):
        n = acc.shape[1]
        rot = acc * jnp.tile(ct, (1, nh)) + pltpu.roll(acc, n - dr, 1) * jnp.tile(st, (1, nh))
        keep = lax.broadcasted_iota(I32, acc.shape, 1) % LANES < dr
        return jnp.where(keep, rot * scale, 0.0)

    qp = _mm(cq, w_qr_ext, extras=[("rows", cos_t), ("rows", sin_t)], epilogue=rope_epilogue,
             out_dtypes=(BF16,), tn=nh * LANES, name="mla_uq_rope")

    o_p = _attn_prompt(qn, qp, c_bf, wuk_t, wuv_t, nb=nb, seq=seq, tq=min(seq, 128), tk=min(seq, 512),
                       scale=scale)
    ns = M - n_prompt
    qa_s = _head_mm(qn[n_prompt:], wuk_t, scale=scale).reshape(ns, nh, dc)
    ol_s = _attn_decode(qa_s, qp[n_prompt:].reshape(ns, nh, LANES), c_bf[n_prompt:].reshape(ns, 1, -1),
                        cache_latent, cache_kpe_t, page_table, dr=dr, npg=min(page_table.shape[1], 32))
    o_s = _head_mm(ol_s.reshape(ns, nh * dc), wuv_t)
    o = jnp.concatenate([o_p, o_s], 0)
    return _mm(o, w_o, layer=li, extras=[("full", h)], epilogue=lambda acc, hh: hh + acc, name="mla_o")


def kernel(x_prompt, x_sample, state_wkv, state_shift, cache_latent, cache_kpe, page_table, p_prompt, p_sample, a_norm, a_mu, a_wr, a_wk, a_wv, a_wo, a_w0, a_w1, a_w2, a_a0, a_a1, a_a2, a_g1, a_g2, a_kk, a_ka, a_rk, a_lnx_w, a_lnx_b, kv_norm, kv_wdown, kv_latent_norm, kv_wuk, kv_wuv, b_norm, b_wdq, b_qnorm, b_wuq, b_wo, f_norm, f_wgroup, f_bgroup, f_wexpert, f_bexpert, f_wgate, f_wup, f_wdown, pl_wproj, pl_norm, pl_wgate, final_norm):
    nb, seq, D = x_prompt.shape
    ns, dec_seq, _ = x_sample.shape
    assert dec_seq == 1
    depth = f_norm.shape[0]
    n_a = a_norm.shape[0]
    n_prompt = nb * seq
    dc = kv_latent_norm.shape[0]
    dr = kv_wdown.shape[1] - dc
    past_len = page_table.shape[1] * cache_latent.shape[1]
    pos = jnp.concatenate([jnp.tile(jnp.arange(seq, dtype=I32), nb), jnp.full((ns,), past_len, I32)])

    h = jnp.concatenate([x_prompt.reshape(n_prompt, D), x_sample.reshape(ns, D)], 0)
    p_all = jnp.concatenate([p_prompt.reshape(depth, n_prompt, -1), p_sample.reshape(depth, ns, -1)], 1)
    wkv_p, wkv_s, sh_p, sh_s = [], [], [], []
    c_f32 = c_bf = None
    cache_kpe_t = jnp.swapaxes(cache_kpe, 1, 2)
    aw = (a_norm, a_mu, a_wr, a_wk, a_wv, a_wo, a_w0, a_w1, a_w2, a_a0, a_a1, a_a2, a_g1, a_g2,
          a_kk, a_ka, a_rk, a_lnx_w, a_lnx_b)
    for i in range(depth):
        if i < n_a:
            h, s_p, s_s, xn = _rwkv_layer(h, state_wkv[i], state_shift[i], n_prompt, nb, seq, aw, i)
            wkv_p.append(s_p)
            wkv_s.append(s_s)
            sh_p.append(xn[seq - 1:n_prompt:seq])
            sh_s.append(xn[n_prompt:])
        else:
            j = i - n_a
            bw = (b_norm[j], b_wdq, b_qnorm[j], b_wuq[j], kv_wuk, kv_wuv, b_wo)
            h = _mla_layer(h, pos, c_bf, n_prompt, nb, seq, cache_latent, cache_kpe_t, page_table, bw, j)
        h = _moe(h, f_norm[i], f_wgroup[i], f_bgroup[i], f_wexpert[i], f_bexpert[i],
                 f_wgate, f_wup, f_wdown, i)
        h = _ple_layer(h, p_all, pl_wproj, pl_norm[i], pl_wgate, i)
        if i == n_a - 1:
            c_f32, c_bf = _shared_kv(h, pos, kv_norm, kv_wdown, kv_latent_norm, dc, dr)
    y_p = _norm(h, final_norm, F32, 0, n_prompt)
    y_s = _norm(h, final_norm, F32, n_prompt, ns)
    lat, kpe = c_f32[:, :dc], c_f32[:, dc:dc + dr]
    return (y_p.reshape(nb, seq, D), y_s.reshape(ns, 1, D),
            jnp.stack(wkv_p), jnp.stack(sh_p),
            lat[:n_prompt].reshape(nb, seq, dc), kpe[:n_prompt].reshape(nb, seq, dr),
            jnp.stack(wkv_s), jnp.stack(sh_s),
            lat[n_prompt:].reshape(ns, 1, dc), kpe[n_prompt:].reshape(ns, 1, dr))
```

```python
import functools

import jax
import jax.numpy as jnp
from jax import lax
from jax.experimental import pallas as pl
from jax.experimental.pallas import tpu as pltpu

F32 = jnp.float32
BF16 = jnp.bfloat16
I32 = jnp.int32

RMS_EPS = 1e-6
GN_EPS = 64e-5
ROPE_THETA = 10000.0
RWKV_HEAD = 64
LANES = 128
SUBLANES = 8
STATE_LANES = 256
HEADS_PER_TILE = STATE_LANES // RWKV_HEAD
ROW_TILE = 128
EXPERT_TILE = 128
TOP_K = 2
VMEM_LIMIT = 56 * 1024 * 1024


def _cparams(*sem):
    return pltpu.CompilerParams(dimension_semantics=sem, vmem_limit_bytes=VMEM_LIMIT)


def _pick_tile(n, target, mult=16):
    best = None
    for t in range(mult, min(n, target) + 1, mult):
        if n % t == 0:
            best = t
    assert best is not None, (n, target, mult)
    return best


def _rms(x, g):
    return x * lax.rsqrt(jnp.mean(x * x, -1, keepdims=True) + RMS_EPS) * g


def _split_bf16(x):
    hi = x.astype(BF16)
    lo = (x - hi.astype(F32)).astype(BF16)
    return hi, lo


def _seg_ones(width=LANES):
    r = lax.broadcasted_iota(I32, (width, width), 0) // RWKV_HEAD
    c = lax.broadcasted_iota(I32, (width, width), 1) // RWKV_HEAD
    return (r == c).astype(BF16)


def _seg_sum(x, seg):
    outs = []
    for j in range(x.shape[1] // LANES):
        hi, lo = _split_bf16(x[:, j * LANES:(j + 1) * LANES])
        outs.append(jnp.dot(hi, seg, preferred_element_type=F32)
                    + jnp.dot(lo, seg, preferred_element_type=F32))
    return jnp.concatenate(outs, axis=1)


def _mm_kernel(*refs, n_extra, n_out, epilogue, normed, n_mm):
    x_refs = refs[0:2 * n_mm:2]
    w_refs = refs[1:2 * n_mm:2]
    extra_refs = refs[2 * n_mm:2 * n_mm + n_extra]
    out_refs = refs[2 * n_mm + n_extra:2 * n_mm + n_extra + n_out]
    wb_refs = refs[2 * n_mm + n_extra + n_out:]

    @pl.when(pl.program_id(1) == 0)
    def _():
        for w_ref, wb_ref in zip(w_refs, wb_refs):
            wb_ref[...] = w_ref[...].astype(BF16)

    x = x_refs[0][...]
    if normed:
        x = _rms(x, extra_refs[-1][...])
        extra_refs = extra_refs[:-1]
    accs = [jnp.dot(x.astype(BF16), wb_refs[0][...], preferred_element_type=F32)]
    for x_ref, wb_ref in zip(x_refs[1:], wb_refs[1:]):
        accs.append(jnp.dot(x_ref[...].astype(BF16), wb_ref[...], preferred_element_type=F32))
    outs = epilogue(*accs, *[r[...] for r in extra_refs]) if epilogue is not None else accs[0]
    if not isinstance(outs, (tuple, list)):
        outs = (outs,)
    for o_ref, o in zip(out_refs, outs):
        o_ref[...] = o.astype(o_ref.dtype)


def _mm(x, w, *, slab=None, layer=None, norm_gain=None, second=None, epilogue=None, extras=(),
        out_dtypes=(F32,), out_cols=None, tn=512, tm_target=832, name="mm"):
    M, K = x.shape[-2:]
    N = w.shape[-1]
    tn = min(tn, N)
    assert N % tn == 0
    tm = _pick_tile(M, tm_target)
    extras = list(extras)
    if norm_gain is not None:
        extras.append(("const", norm_gain.reshape(1, K)))
    out_cols = out_cols or [tn] * len(out_dtypes)
    def operand_specs(xa, wa, sl, ly):
        ka = xa.shape[-1]
        if xa.ndim == 3:
            xs = pl.BlockSpec((None, tm, ka), lambda j, i: (sl, i, 0))
        else:
            xs = pl.BlockSpec((tm, ka), lambda j, i: (i, 0))
        if wa.ndim == 3:
            ws = pl.BlockSpec((None, ka, tn), lambda j, i: (ly, 0, j))
        else:
            ws = pl.BlockSpec((ka, tn), lambda j, i: (0, j))
        return [xs, ws]

    in_specs = operand_specs(x, w, slab, layer)
    args = [x, w]
    scratch = [pltpu.VMEM((K, tn), BF16)]
    if second is not None:
        x2, w2, slab2, layer2 = second
        in_specs += operand_specs(x2, w2, slab2, layer2)
        args += [x2, w2]
        scratch.append(pltpu.VMEM((x2.shape[-1], tn), BF16))
    for kind, arr in extras:
        if kind == "row":
            in_specs.append(pl.BlockSpec((1, tn), lambda j, i: (0, j)))
        elif kind == "full":
            in_specs.append(pl.BlockSpec((tm, tn), lambda j, i: (i, j)))
        elif kind == "rows":
            in_specs.append(pl.BlockSpec((tm, arr.shape[1]), lambda j, i: (i, 0)))
        elif kind == "const":
            in_specs.append(pl.BlockSpec(arr.shape, lambda j, i: (0,) * arr.ndim))
        else:
            raise ValueError(kind)
        args.append(arr)
    nj = N // tn
    out_shape = [jax.ShapeDtypeStruct((M, oc * nj), dt) for dt, oc in zip(out_dtypes, out_cols)]
    out_specs = [pl.BlockSpec((tm, oc), lambda j, i: (i, j)) for oc in out_cols]
    res = pl.pallas_call(
        functools.partial(_mm_kernel, n_extra=len(extras), n_out=len(out_dtypes), epilogue=epilogue,
                          normed=norm_gain is not None, n_mm=len(scratch)),
        out_shape=out_shape,
        grid=(nj, M // tm),
        in_specs=in_specs,
        out_specs=out_specs,
        scratch_shapes=scratch,
        compiler_params=_cparams("parallel", "arbitrary"),
        name=name,
    )(*args)
    return res[0] if len(res) == 1 else res


def _norm_kernel(h_ref, g_ref, o_ref):
    o_ref[...] = _rms(h_ref[...], g_ref[...]).astype(o_ref.dtype)


def _norm(h, g, out_dtype, row0=0, nrows=None):
    D = h.shape[1]
    M = h.shape[0] - row0 if nrows is None else nrows
    assert row0 % ROW_TILE == 0 and M % ROW_TILE == 0
    base = row0 // ROW_TILE
    return pl.pallas_call(
        _norm_kernel,
        out_shape=jax.ShapeDtypeStruct((M, D), out_dtype),
        grid=(M // ROW_TILE,),
        in_specs=[pl.BlockSpec((ROW_TILE, D), lambda i: (base + i, 0)),
                  pl.BlockSpec((1, D), lambda i: (0, 0))],
        out_specs=pl.BlockSpec((ROW_TILE, D), lambda i: (i, 0)),
        compiler_params=_cparams("parallel"),
        name="rmsnorm",
    )(h, g.reshape(1, D))


def _mix_kernel(h_ref, g_ref, mu_ref, sp_ref, xm_ref, xn_ref, carry_ref, *, tiles_per_seq, n_prompt_tiles):
    i = pl.program_id(0)
    xn = _rms(h_ref[...], g_ref[...])
    xn_ref[...] = xn
    prev = jnp.where(i % tiles_per_seq == 0, 0.0, carry_ref[...])
    row = lax.broadcasted_iota(I32, xn.shape, 0)
    shifted = jnp.where(row == 0, prev, pltpu.roll(xn, 1, 0))
    shifted = jnp.where(i >= n_prompt_tiles, sp_ref[...], shifted)
    carry_ref[...] = xn[ROW_TILE - 1:ROW_TILE, :]
    dx = shifted - xn
    for j in range(mu_ref.shape[0]):
        xm_ref[j] = (xn + dx * mu_ref[j:j + 1, :]).astype(BF16)


def _norm_shift_mix(h, g, mu, state_shift, seq, n_prompt):
    M, D = h.shape
    assert seq % ROW_TILE == 0 and n_prompt % ROW_TILE == 0 and (M - n_prompt) % ROW_TILE == 0
    npt = n_prompt // ROW_TILE
    nmix = mu.shape[0]
    return pl.pallas_call(
        functools.partial(_mix_kernel, tiles_per_seq=seq // ROW_TILE, n_prompt_tiles=npt),
        out_shape=[jax.ShapeDtypeStruct((nmix, M, D), BF16), jax.ShapeDtypeStruct((M, D), F32)],
        grid=(M // ROW_TILE,),
        in_specs=[pl.BlockSpec((ROW_TILE, D), lambda i: (i, 0)),
                  pl.BlockSpec((1, D), lambda i: (0, 0)),
                  pl.BlockSpec((nmix, D), lambda i: (0, 0)),
                  pl.BlockSpec((ROW_TILE, D), lambda i: (jnp.maximum(i - npt, 0), 0))],
        out_specs=[pl.BlockSpec((nmix, ROW_TILE, D), lambda i: (0, i, 0)),
                   pl.BlockSpec((ROW_TILE, D), lambda i: (i, 0))],
        scratch_shapes=[pltpu.VMEM((1, D), F32)],
        compiler_params=_cparams("arbitrary"),
        name="norm_shift_mix",
    )(h, g.reshape(1, D), mu, state_shift)


def _wkv_post_kernel(*refs):
    y_refs = refs[:HEADS_PER_TILE]
    r_ref, k_ref, v_ref, g_ref, lw_ref, lb_ref, rk_ref, o_ref = refs[HEADS_PER_TILE:]
    seg = _seg_ones()
    inv = 1.0 / RWKV_HEAD
    y = _heads_from_parts([y_ref[...] for y_ref in y_refs], r_ref.shape[1] // RWKV_HEAD)
    mean = _seg_sum(y, seg) * inv
    d = y - mean
    var = _seg_sum(d * d, seg) * inv
    yn = d * lax.rsqrt(var + GN_EPS) * lw_ref[...] + lb_ref[...]
    v = v_ref[...]
    bonus = _seg_sum(r_ref[...] * k_ref[...] * rk_ref[...], seg) * v
    o_ref[...] = ((yn + bonus) * g_ref[...]).astype(o_ref.dtype)


def _wkv_post(y_parts, r, k2, v, g, lnx_w, lnx_b, r_k):
    M, D = r.shape
    big = pl.BlockSpec((ROW_TILE, D), lambda i: (i, 0))
    part = pl.BlockSpec((ROW_TILE, D // HEADS_PER_TILE), lambda i: (i, 0))
    row = pl.BlockSpec((1, D), lambda i: (0, 0))
    return pl.pallas_call(
        _wkv_post_kernel,
        out_shape=jax.ShapeDtypeStruct((M, D), BF16),
        grid=(M // ROW_TILE,),
        in_specs=[part] * HEADS_PER_TILE + [big, big, big, big, row, row, row],
        out_specs=big,
        compiler_params=_cparams("parallel"),
        name="wkv_post",
    )(*y_parts, r, k2, v, g, lnx_w.reshape(1, D), lnx_b.reshape(1, D), r_k.reshape(1, D))


def _wkv_consts(ntile):
    seg = _seg_ones(STATE_LANES)
    rows = ntile * RWKV_HEAD
    rr = lax.broadcasted_iota(I32, (rows, STATE_LANES), 0) % RWKV_HEAD
    cc = lax.broadcasted_iota(I32, (rows, STATE_LANES), 1) % RWKV_HEAD
    diag = rr == cc
    eh = (lax.broadcasted_iota(I32, (SUBLANES, STATE_LANES), 0)
          == lax.broadcasted_iota(I32, (SUBLANES, STATE_LANES), 1) // RWKV_HEAD).astype(BF16)
    return seg, diag, eh


def _bcast_row(x, i, ntile):
    return jnp.concatenate(
        [jnp.broadcast_to(x[i:i + 1, j * STATE_LANES:(j + 1) * STATE_LANES], (RWKV_HEAD, STATE_LANES))
         for j in range(ntile)], axis=0)


def _wkv_step(S, vals, i, ntile, consts):
    r8, w8, k8, v8, a8, b8 = vals
    seg, diag, eh = consts
    sa = jnp.dot((S * _bcast_row(a8, i, ntile)).astype(BF16), seg, preferred_element_type=F32)
    vb = jnp.dot(jnp.where(diag, _bcast_row(v8, i, ntile), 0.0).astype(BF16), seg, preferred_element_type=F32)
    s_new = S * _bcast_row(w8, i, ntile) + sa * _bcast_row(b8, i, ntile) + vb * _bcast_row(k8, i, ntile)
    p = (s_new * _bcast_row(r8, i, ntile)).astype(BF16)
    y_t = lax.dot_general(eh, p, (((1,), (1,)), ((), ())), preferred_element_type=F32)
    return s_new, y_t


def _wkv_seq_kernel(*refs, nb, tc, ntile):
    in_refs = refs[:6 * nb]
    s0_ref = refs[6 * nb]
    y_refs = refs[6 * nb + 1:6 * nb + 1 + HEADS_PER_TILE]
    s_ref = refs[6 * nb + 1 + HEADS_PER_TILE]
    consts = _wkv_consts(ntile)

    @pl.when(pl.program_id(1) == 0)
    def _():
        s_ref[...] = s0_ref[...]

    def body(t8, carry):
        base = pl.multiple_of(t8 * SUBLANES, SUBLANES)
        vals = [[ref[pl.ds(base, SUBLANES), :] for ref in in_refs[6 * u:6 * u + 6]] for u in range(nb)]
        S = [s_ref[u] for u in range(nb)]
        ys = [[] for _ in range(nb)]
        for i in range(SUBLANES):
            for u in range(nb):
                S[u], y_t = _wkv_step(S[u], vals[u], i, ntile, consts)
                ys[u].append(y_t)
        for u in range(nb):
            s_ref[u] = S[u]
            for hs in range(HEADS_PER_TILE):
                y_refs[hs][u, pl.ds(base, SUBLANES), :] = jnp.concatenate(
                    [y[hs:hs + 1, :] for y in ys[u]], axis=0)
        return carry

    lax.fori_loop(0, tc // SUBLANES, body, 0)


def _wkv_scan_seq(rows, s0, *, nseq, seq, row0, nb, tc):
    D = rows[0].shape[1]
    ntile = D // STATE_LANES
    assert nseq % nb == 0 and seq % tc == 0 and row0 % tc == 0 and tc % SUBLANES == 0
    nt = seq // tc
    base = row0 // tc
    in_specs, args = [], []
    for u in range(nb):
        for arr in rows:
            in_specs.append(pl.BlockSpec((tc, D), lambda g, c, u=u: (base + (g * nb + u) * nt + c, 0)))
            args.append(arr)
    srows = ntile * RWKV_HEAD
    in_specs.append(pl.BlockSpec((nb, srows, STATE_LANES), lambda g, c: (g, 0, 0)))
    args.append(s0)
    half = D // HEADS_PER_TILE
    y_spec = pl.BlockSpec((nb, tc, half), lambda g, c: (g, c, 0))
    res = pl.pallas_call(
        functools.partial(_wkv_seq_kernel, nb=nb, tc=tc, ntile=ntile),
        out_shape=[jax.ShapeDtypeStruct((nseq, seq, half), F32)] * HEADS_PER_TILE
        + [jax.ShapeDtypeStruct((nseq, srows, STATE_LANES), F32)],
        grid=(nseq // nb, nt),
        in_specs=in_specs,
        out_specs=[y_spec] * HEADS_PER_TILE + [pl.BlockSpec((nb, srows, STATE_LANES), lambda g, c: (g, 0, 0))],
        compiler_params=_cparams("parallel", "arbitrary"),
        name="wkv_scan_seq",
    )(*args)
    return res[:HEADS_PER_TILE], res[HEADS_PER_TILE]


def _wkv_one_kernel(*refs, nu, ntile):
    in_refs = refs[:6]
    s0_ref = refs[6]
    y_refs = refs[7:7 + HEADS_PER_TILE]
    s_ref = refs[7 + HEADS_PER_TILE]
    consts = _wkv_consts(ntile)
    vals = [ref[...] for ref in in_refs]
    ys = []
    for u in range(nu):
        s_new, y_t = _wkv_step(s0_ref[u], vals, u, ntile, consts)
        s_ref[u] = s_new
        ys.append(y_t)
    for hs in range(HEADS_PER_TILE):
        y_refs[hs][...] = jnp.concatenate([y[hs:hs + 1, :] for y in ys], axis=0)


def _wkv_scan_one(rows, s0, *, nseq, row0, nu=SUBLANES):
    D = rows[0].shape[1]
    ntile = D // STATE_LANES
    assert nu == SUBLANES and nseq % nu == 0 and row0 % nu == 0
    base = row0 // nu
    srows = ntile * RWKV_HEAD
    half = D // HEADS_PER_TILE
    in_specs = [pl.BlockSpec((nu, D), lambda g: (base + g, 0)) for _ in rows]
    in_specs.append(pl.BlockSpec((nu, srows, STATE_LANES), lambda g: (g, 0, 0)))
    y_spec = pl.BlockSpec((nu, half), lambda g: (g, 0))
    res = pl.pallas_call(
        functools.partial(_wkv_one_kernel, nu=nu, ntile=ntile),
        out_shape=[jax.ShapeDtypeStruct((nseq, half), F32)] * HEADS_PER_TILE
        + [jax.ShapeDtypeStruct((nseq, srows, STATE_LANES), F32)],
        grid=(nseq // nu,),
        in_specs=in_specs,
        out_specs=[y_spec] * HEADS_PER_TILE + [pl.BlockSpec((nu, srows, STATE_LANES), lambda g: (g, 0, 0))],
        compiler_params=_cparams("parallel"),
        name="wkv_scan_one",
    )(*rows, s0)
    return res[:HEADS_PER_TILE], res[HEADS_PER_TILE]


def _state_to_tiles(s):
    n, H = s.shape[0], s.shape[1]
    s = s.reshape(n, H // HEADS_PER_TILE, HEADS_PER_TILE, RWKV_HEAD, RWKV_HEAD)
    return jnp.transpose(s, (0, 1, 3, 2, 4)).reshape(n, H // HEADS_PER_TILE * RWKV_HEAD, STATE_LANES)


def _tiles_to_state(s, H):
    n = s.shape[0]
    s = s.reshape(n, H // HEADS_PER_TILE, RWKV_HEAD, HEADS_PER_TILE, RWKV_HEAD)
    return jnp.transpose(s, (0, 1, 3, 2, 4)).reshape(n, H, RWKV_HEAD, RWKV_HEAD)


def _heads_from_parts(parts, H):
    return jnp.concatenate(
        [parts[h % HEADS_PER_TILE][:, (h // HEADS_PER_TILE) * RWKV_HEAD:(h // HEADS_PER_TILE + 1) * RWKV_HEAD]
         for h in range(H)], axis=1)


def _router_kernel(h_ref, g_ref, w_ref, b_ref, gid_ref, xb_ref, info_ref, wh_ref, wl_ref, *, n_groups):
    @pl.when(pl.program_id(0) == 0)
    def _():
        wh, wl = _split_bf16(w_ref[...])
        wh_ref[...] = wh
        wl_ref[...] = wl

    xn = _rms(h_ref[...], g_ref[...])
    xh, xl = _split_bf16(xn)
    xb_ref[...] = xn
    logits = (jnp.dot(xh, wh_ref[...], preferred_element_type=F32)
              + jnp.dot(xl, wh_ref[...], preferred_element_type=F32)
              + jnp.dot(xh, wl_ref[...], preferred_element_type=F32)) + b_ref[...]
    lane = lax.broadcasted_iota(I32, logits.shape, 1)
    lanef = lane.astype(F32)
    neg = -jnp.inf
    big = float(LANES)
    lg = jnp.where(lane < n_groups, logits, neg)
    mg = jnp.max(lg, -1, keepdims=True)
    g_top = jnp.min(jnp.where(lg == mg, lanef, big), -1, keepdims=True)
    p_sel = 1.0 / jnp.sum(jnp.exp(lg - mg), -1, keepdims=True)
    le = jnp.where(gid_ref[...] == g_top, logits, neg)
    v1 = jnp.max(le, -1, keepdims=True)
    i1 = jnp.min(jnp.where(le == v1, lanef, big), -1, keepdims=True)
    le2 = jnp.where(lanef == i1, neg, le)
    v2 = jnp.max(le2, -1, keepdims=True)
    i2 = jnp.min(jnp.where(le2 == v2, lanef, big), -1, keepdims=True)
    e2 = jnp.exp(v2 - v1)
    den = 1.0 + e2
    gate1 = (1.0 / den) * p_sel
    gate2 = (e2 / den) * p_sel
    info = jnp.where(lane == 0, i1 - n_groups,
                     jnp.where(lane == 1, i2 - n_groups,
                               jnp.where(lane == 2, gate1, jnp.where(lane == 3, gate2, 0.0))))
    info_ref[...] = info


def _router(h, g, w_route, b_route, n_groups, per_group):
    M, D = h.shape
    lane = jnp.arange(LANES, dtype=I32)
    is_expert = (lane >= n_groups) & (lane < n_groups * (1 + per_group))
    gid = jnp.where(is_expert, (lane - n_groups) // per_group, -1).astype(F32).reshape(1, LANES)
    return pl.pallas_call(
        functools.partial(_router_kernel, n_groups=n_groups),
        out_shape=[jax.ShapeDtypeStruct((M, D), F32), jax.ShapeDtypeStruct((M, LANES), F32)],
        grid=(M // ROW_TILE,),
        in_specs=[pl.BlockSpec((ROW_TILE, D), lambda i: (i, 0)),
                  pl.BlockSpec((1, D), lambda i: (0, 0)),
                  pl.BlockSpec((D, LANES), lambda i: (0, 0)),
                  pl.BlockSpec((1, LANES), lambda i: (0, 0)),
                  pl.BlockSpec((1, LANES), lambda i: (0, 0))],
        out_specs=[pl.BlockSpec((ROW_TILE, D), lambda i: (i, 0)),
                   pl.BlockSpec((ROW_TILE, LANES), lambda i: (i, 0))],
        scratch_shapes=[pltpu.VMEM((D, LANES), BF16), pltpu.VMEM((D, LANES), BF16)],
        compiler_params=_cparams("arbitrary"),
        name="moe_router",
    )(h, g.reshape(1, D), w_route, b_route, gid)


def _expert_kernel(te_ref, used_ref, tok_ref, x_hbm, gate_ref, wg_ref, wu_ref, wd_ref, o_ref,
                   xbuf, sem, wgb, wub, wdb):
    i = pl.program_id(0)
    used = used_ref[0]
    slot = i % 2
    new_expert = jnp.logical_or(i == 0, te_ref[i] != te_ref[jnp.maximum(i - 1, 0)])

    def row_copy(tile, r, sl):
        tok = tok_ref[tile * EXPERT_TILE + r]
        return pltpu.make_async_copy(x_hbm.at[pl.ds(tok, 1), :], xbuf.at[sl, pl.ds(r, 1), :], sem.at[sl])

    def start_tile(tile, sl):
        for r in range(EXPERT_TILE):
            row_copy(tile, r, sl).start(priority=r % 2)

    @pl.when(jnp.logical_and(i == 0, used > 0))
    def _():
        start_tile(0, 0)

    @pl.when(i + 1 < used)
    def _():
        start_tile(i + 1, 1 - slot)

    @pl.when(jnp.logical_and(i < used, new_expert))
    def _():
        wgb[...] = wg_ref[0].astype(BF16)
        wub[...] = wu_ref[0].astype(BF16)
        wdb[...] = wd_ref[0].astype(BF16)

    @pl.when(i < used)
    def _():
        for r in range(EXPERT_TILE):
            row_copy(i, r, slot).wait()
        x = xbuf[slot].astype(BF16)
        hg = jnp.dot(x, wgb[...], preferred_element_type=F32)
        hu = jnp.dot(x, wub[...], preferred_element_type=F32)
        hid = (hg * jax.nn.sigmoid(hg)) * hu * gate_ref[...]
        o_ref[...] = jnp.dot(hid.astype(BF16), wdb[...], preferred_element_type=F32)

    @pl.when(i >= used)
    def _():
        o_ref[...] = jnp.zeros_like(o_ref)


def _experts(tile_expert, used, row_token, x, row_gate, w_gate, w_up, w_down, layer):
    R = row_token.shape[0]
    D = x.shape[1]
    _, E, _, Fd = w_gate.shape
    nt = R // EXPERT_TILE
    return pl.pallas_call(
        _expert_kernel,
        out_shape=jax.ShapeDtypeStruct((R, D), F32),
        grid_spec=pltpu.PrefetchScalarGridSpec(
            num_scalar_prefetch=3,
            grid=(nt,),
            in_specs=[pl.BlockSpec(memory_space=pl.ANY),
                      pl.BlockSpec((EXPERT_TILE, 1), lambda i, te, u, tk: (i, 0)),
                      pl.BlockSpec((None, 1, D, Fd), lambda i, te, u, tk: (layer, te[i], 0, 0)),
                      pl.BlockSpec((None, 1, D, Fd), lambda i, te, u, tk: (layer, te[i], 0, 0)),
                      pl.BlockSpec((None, 1, Fd, D), lambda i, te, u, tk: (layer, te[i], 0, 0))],
            out_specs=pl.BlockSpec((EXPERT_TILE, D), lambda i, te, u, tk: (i, 0)),
            scratch_shapes=[pltpu.VMEM((2, EXPERT_TILE, D), F32), pltpu.SemaphoreType.DMA((2,)),
                            pltpu.VMEM((D, Fd), BF16), pltpu.VMEM((D, Fd), BF16), pltpu.VMEM((Fd, D), BF16)],
        ),
        compiler_params=_cparams("arbitrary"),
        name="moe_experts",
    )(tile_expert, used, row_token, x, row_gate, w_gate, w_up, w_down)


def _moe(h, f_norm, w_group, b_group, w_expert, b_expert, w_gate, w_up, w_down, layer):
    M, D = h.shape
    G = w_group.shape[1]
    P = w_expert.shape[2]
    E = G * P
    assert G + E <= LANES
    w_route = jnp.zeros((D, LANES), F32)
    w_route = w_route.at[:, :G].set(w_group)
    w_route = w_route.at[:, G:G + E].set(jnp.transpose(w_expert, (1, 0, 2)).reshape(D, E))
    b_route = jnp.zeros((1, LANES), F32)
    b_route = b_route.at[0, :G].set(b_group)
    b_route = b_route.at[0, G:G + E].set(b_expert.reshape(E))
    xn, info = _router(h, f_norm, w_route, b_route, G, P)

    eid = info[:, :TOP_K].astype(I32).reshape(-1)
    gates = info[:, TOP_K:2 * TOP_K].reshape(-1)
    npair = M * TOP_K
    nt = -(-npair // EXPERT_TILE) + E
    R = nt * EXPERT_TILE
    onehot = (eid[:, None] == jnp.arange(E, dtype=I32)[None, :]).astype(I32)
    csum = jnp.cumsum(onehot, axis=0)
    counts = csum[-1]
    rank = jnp.sum(csum * onehot, axis=1) - 1
    padded = (counts + EXPERT_TILE - 1) // EXPERT_TILE * EXPERT_TILE
    pend = jnp.cumsum(padded)
    pstart = pend - padded
    dest = jnp.sum(onehot * pstart[None, :], axis=1) + rank
    token = (jnp.arange(npair, dtype=I32) // TOP_K).astype(F32)
    rows = jnp.zeros((R, 2), F32).at[dest].set(jnp.stack([token, gates], -1))
    row_token = rows[:, 0].astype(I32)
    tile_row0 = jnp.arange(nt, dtype=I32) * EXPERT_TILE
    tile_expert = jnp.minimum(jnp.sum((pend[None, :] <= tile_row0[:, None]).astype(I32), axis=1), E - 1)
    used = (pend[-1] // EXPERT_TILE).astype(I32).reshape(1)

    ys = _experts(tile_expert, used, row_token, xn, rows[:, 1:2], w_gate, w_up, w_down, layer)
    dest = dest.reshape(M, TOP_K)
    y = jnp.take(ys, dest[:, 0], axis=0)
    for s in range(1, TOP_K):
        y = y + jnp.take(ys, dest[:, s], axis=0)
    return h + y


def _row_reduce(x, combine, reduce):
    t = x[:, 0:LANES]
    for j in range(1, x.shape[1] // LANES):
        t = combine(t, x[:, j * LANES:(j + 1) * LANES])
    return reduce(t, -1, keepdims=True)


def _attn_prompt_kernel(qn_ref, qp_ref, kv_ref, wuk_ref, wuv_ref, o_ref, q_s, m_s, l_s, acc_s,
                        *, tq, tk, nh, dn, dc, scale):
    qi = pl.program_id(1)
    kv = pl.program_id(2)
    last = (qi * tq + tq - 1) // tk
    rows = nh * tq

    @pl.when(kv == 0)
    def _():
        for h in range(nh):
            qa = jnp.dot(qn_ref[:, h * dn:(h + 1) * dn], wuk_ref[h], preferred_element_type=F32) * scale
            q_s[h * tq:(h + 1) * tq, 0:dc] = qa.astype(BF16)
            q_s[h * tq:(h + 1) * tq, dc:] = qp_ref[:, h * LANES:(h + 1) * LANES]
        m_s[...] = jnp.full(m_s.shape, -jnp.inf, F32)
        l_s[...] = jnp.zeros(l_s.shape, F32)
        acc_s[...] = jnp.zeros(acc_s.shape, F32)

    def update(masked):
        kblk = kv_ref[...]
        s = lax.dot_general(q_s[...], kblk, (((1,), (1,)), ((), ())), preferred_element_type=F32)
        if masked:
            qpos = qi * tq + lax.broadcasted_iota(I32, (rows, tk), 0) % tq
            kpos = kv * tk + lax.broadcasted_iota(I32, (rows, tk), 1)
            s = jnp.where(kpos <= qpos, s, -jnp.inf)
        m_old = m_s[...]
        m_new = jnp.maximum(m_old, _row_reduce(s, jnp.maximum, jnp.max))
        alpha = jnp.exp(m_old - m_new)
        p = jnp.exp(s - m_new)
        l_s[...] = alpha * l_s[...] + _row_reduce(p, jnp.add, jnp.sum)
        acc_s[...] = alpha * acc_s[...] + jnp.dot(p.astype(BF16), kblk[:, 0:dc], preferred_element_type=F32)
        m_s[...] = m_new

    @pl.when(kv < last)
    def _():
        update(False)

    @pl.when(kv == last)
    def _():
        update(True)
        o = acc_s[...] / l_s[...]
        for h in range(nh):
            oh = jnp.dot(o[h * tq:(h + 1) * tq].astype(BF16), wuv_ref[h], preferred_element_type=F32)
            o_ref[:, h * oh.shape[1]:(h + 1) * oh.shape[1]] = oh.astype(o_ref.dtype)


def _attn_prompt(qn, qp, kvb, wuk_t, wuv_t, *, nb, seq, tq, tk, scale):
    nh, dn, dc = wuk_t.shape
    dv = wuv_t.shape[2]
    width = kvb.shape[1]
    assert width == dc + LANES and qp.shape[1] == nh * LANES
    nq, nk = seq // tq, seq // tk

    def kv_map(b, qi, kv):
        return (b * nk + jnp.minimum(kv, (qi * tq + tq - 1) // tk), 0)

    return pl.pallas_call(
        functools.partial(_attn_prompt_kernel, tq=tq, tk=tk, nh=nh, dn=dn, dc=dc, scale=scale),
        out_shape=jax.ShapeDtypeStruct((nb * seq, nh * dv), BF16),
        grid=(nb, nq, nk),
        in_specs=[pl.BlockSpec((tq, nh * dn), lambda b, qi, kv: (b * nq + qi, 0)),
                  pl.BlockSpec((tq, nh * LANES), lambda b, qi, kv: (b * nq + qi, 0)),
                  pl.BlockSpec((tk, width), kv_map),
                  pl.BlockSpec((nh, dn, dc), lambda b, qi, kv: (0, 0, 0)),
                  pl.BlockSpec((nh, dc, dv), lambda b, qi, kv: (0, 0, 0))],
        out_specs=pl.BlockSpec((tq, nh * dv), lambda b, qi, kv: (b * nq + qi, 0)),
        scratch_shapes=[pltpu.VMEM((nh * tq, width), BF16),
                        pltpu.VMEM((nh * tq, 1), F32),
                        pltpu.VMEM((nh * tq, 1), F32),
                        pltpu.VMEM((nh * tq, dc), F32)],
        compiler_params=_cparams("parallel", "parallel", "arbitrary"),
        name="mla_prompt_attn",
    )(qn, qp, kvb, wuk_t, wuv_t)


def _attn_decode_kernel(pt_ref, *refs, npg, dr, dc):
    qa_ref, qr_ref, kvn_ref = refs[:3]
    lat_hbm, kpe_hbm, o_ref, m_s, l_s, acc_s, lat_s, kpe_s, latbuf, kpebuf, sem = refs[3:]
    g = pl.program_id(1)
    ng = pl.num_programs(1)
    t = pl.program_id(0) * ng + g
    total = pl.num_programs(0) * ng
    slot = t % 2

    def copies(step, sl):
        out = []
        for i in range(npg):
            pg = pt_ref[step * npg + i]
            out.append(pltpu.make_async_copy(lat_hbm.at[pg], latbuf.at[sl, i], sem.at[0, sl]))
            out.append(pltpu.make_async_copy(kpe_hbm.at[pg], kpebuf.at[sl, i], sem.at[1, sl]))
        return out

    @pl.when(t == 0)
    def _():
        for c in copies(0, 0):
            c.start()

    nxt = jnp.minimum(t + 1, total - 1)
    for c in copies(nxt, 1 - slot):
        c.start()
    for c in copies(t, slot):
        c.wait()

    @pl.when(g == 0)
    def _():
        m_s[...] = jnp.full(m_s.shape, -jnp.inf, F32)
        l_s[...] = jnp.zeros(l_s.shape, F32)
        acc_s[...] = jnp.zeros(acc_s.shape, F32)

    qa = qa_ref[0]
    qr = qr_ref[0][:, 0:dr]
    nt = (((1,), (1,)), ((), ()))
    page = latbuf.shape[2]
    for i in range(npg):
        lat_s[i * page:(i + 1) * page, :] = latbuf[slot, i].astype(BF16)
        kpe_s[:, i * page:(i + 1) * page] = kpebuf[slot, i].astype(BF16)
    lat = lat_s[...]
    s = (lax.dot_general(qa, lat, nt, preferred_element_type=F32)
         + jnp.dot(qr, kpe_s[...], preferred_element_type=F32))
    m_old = m_s[...]
    m_new = jnp.maximum(m_old, jnp.max(s, -1, keepdims=True))
    alpha = jnp.exp(m_old - m_new)
    p = jnp.exp(s - m_new)
    l_s[...] = alpha * l_s[...] + jnp.sum(p, -1, keepdims=True)
    acc_s[...] = alpha * acc_s[...] + jnp.dot(p.astype(BF16), lat, preferred_element_type=F32)
    m_s[...] = m_new

    @pl.when(g == pl.num_programs(1) - 1)
    def _():
        kvn = kvn_ref[0].astype(F32)
        s = (jnp.sum(qa.astype(F32) * kvn[:, 0:dc], -1, keepdims=True)
             + jnp.sum(qr.astype(F32) * kvn[:, dc:dc + dr], -1, keepdims=True))
        m_old = m_s[...]
        m_new = jnp.maximum(m_old, s)
        alpha = jnp.exp(m_old - m_new)
        p = jnp.exp(s - m_new)
        l = alpha * l_s[...] + p
        acc = alpha * acc_s[...] + p * kvn[:, 0:dc]
        o_ref[0] = (acc / l).astype(o_ref.dtype)

    @pl.when(t == total - 1)
    def _():
        for c in copies(nxt, 1 - slot):
            c.wait()


def _attn_decode(qa, qr, kvn, cache_latent, cache_kpe_t, page_table, *, dr, npg):
    nseq, npages = page_table.shape
    _, nh, dc = qa.shape
    width = kvn.shape[2]
    page = cache_latent.shape[1]
    assert npages % npg == 0

    def seq_spec(a, b):
        return pl.BlockSpec((1, a, b), lambda s, g, pt: (s, 0, 0))

    hbm = pl.BlockSpec(memory_space=pl.ANY)
    in_specs = [seq_spec(nh, dc), seq_spec(nh, LANES), seq_spec(1, width), hbm, hbm]
    return pl.pallas_call(
        functools.partial(_attn_decode_kernel, npg=npg, dr=dr, dc=dc),
        out_shape=jax.ShapeDtypeStruct((nseq, nh, dc), BF16),
        grid_spec=pltpu.PrefetchScalarGridSpec(
            num_scalar_prefetch=1,
            grid=(nseq, npages // npg),
            in_specs=in_specs,
            out_specs=seq_spec(nh, dc),
            scratch_shapes=[pltpu.VMEM((nh, 1), F32), pltpu.VMEM((nh, 1), F32), pltpu.VMEM((nh, dc), F32),
                            pltpu.VMEM((npg * page, dc), BF16), pltpu.VMEM((dr, npg * page), BF16),
                            pltpu.VMEM((2, npg, page, dc), F32), pltpu.VMEM((2, npg, dr, page), F32),
                            pltpu.SemaphoreType.DMA((2, 2))],
        ),
        compiler_params=_cparams("arbitrary", "arbitrary"),
        name="mla_decode_attn",
    )(page_table.reshape(-1), qa, qr, kvn, cache_latent, cache_kpe_t)


def _head_mm_kernel(x_ref, w_ref, o_ref, *, scale):
    acc = jnp.dot(x_ref[...], w_ref[0], preferred_element_type=F32)
    if scale is not None:
        acc = acc * scale
    o_ref[...] = acc.astype(o_ref.dtype)


def _head_mm(x, w, scale=None):
    R = x.shape[0]
    nh, kin, kout = w.shape
    return pl.pallas_call(
        functools.partial(_head_mm_kernel, scale=scale),
        out_shape=jax.ShapeDtypeStruct((R, nh * kout), BF16),
        grid=(nh,),
        in_specs=[pl.BlockSpec((R, kin), lambda h: (0, h)),
                  pl.BlockSpec((1, kin, kout), lambda h: (h, 0, 0))],
        out_specs=pl.BlockSpec((R, kout), lambda h: (0, h)),
        compiler_params=_cparams("parallel"),
        name="head_mm",
    )(x, w)


def _rope_tables(pos, dr):
    half = dr // 2
    inv = ROPE_THETA ** (-2.0 * jnp.arange(half, dtype=F32) / dr)
    ang = pos.astype(F32)[:, None] * inv[None, :]
    cos, sin = jnp.cos(ang), jnp.sin(ang)
    rep = LANES // dr
    cos_t = jnp.tile(jnp.concatenate([cos, cos], -1), (1, rep))
    sin_t = jnp.tile(jnp.concatenate([-sin, sin], -1), (1, rep))
    return cos_t, sin_t


def _swap_halves(w, dr):
    return jnp.concatenate([w[..., dr // 2:], w[..., :dr // 2]], -1)


def _rwkv_layer(h, state_wkv, state_shift, n_prompt, nb, seq, aw, li):
    (a_norm, mu, w_r, w_k, w_v, w_o, w0, w1, w2, a0, a1, a2, g1, g2, k_k, k_a, r_k, lnx_w, lnx_b) = aw
    M, D = h.shape
    H = D // RWKV_HEAD
    nsample = M - n_prompt
    row = lambda x: ("row", x[li].reshape(1, D))
    xm, xn = _norm_shift_mix(h, a_norm[li], mu[li], state_shift, seq, n_prompt)
    jr, jw, jk, jv, ja, jg = range(6)

    r = _mm(xm, w_r, slab=jr, layer=li, name="rwkv_r")
    v = _mm(xm, w_v, slab=jv, layer=li, name="rwkv_v")
    th = _mm(xm, w1, slab=jw, layer=li, epilogue=jnp.tanh, out_dtypes=(BF16,), name="rwkv_w1")

    def decay_epilogue(z, w0r):
        u = -(w0r + z)
        softplus = jnp.maximum(u, 0.0) + jnp.log(1.0 + jnp.exp(-jnp.abs(u)))
        return jnp.exp(-jnp.exp(-softplus - 0.5))

    decay = _mm(th, w2, layer=li, extras=[row(w0)], epilogue=decay_epilogue, name="rwkv_w2")
    al = _mm(xm, a1, slab=ja, layer=li, out_dtypes=(BF16,), name="rwkv_a1")
    a_lr = _mm(al, a2, layer=li, extras=[row(a0)], name="rwkv_a2",
               epilogue=lambda z, a0r: jax.nn.sigmoid(a0r + z))
    gl = _mm(xm, g1, slab=jg, layer=li, epilogue=jax.nn.sigmoid, out_dtypes=(BF16,), name="rwkv_g1")
    g = _mm(gl, g2, layer=li, name="rwkv_g2")

    def k_epilogue(k, a, kkr, kar):
        seg = _seg_ones()
        kk = k * kkr
        nrm = jnp.sqrt(_seg_sum(kk * kk, seg))
        kk = kk / jnp.maximum(nrm, 1e-12)
        return k * (1.0 + (a - 1.0) * kar), -kk, kk * a

    k2, a_neg, b_pos = _mm(xm, w_k, slab=jk, layer=li, extras=[("full", a_lr), row(k_k), row(k_a)],
                           epilogue=k_epilogue, out_dtypes=(F32, F32, F32), name="rwkv_k")

    rows = (r, decay, k2, v, a_neg, b_pos)
    s0_p = jnp.zeros((nb, H // HEADS_PER_TILE * RWKV_HEAD, STATE_LANES), F32)
    yp_p, s_p = _wkv_scan_seq(rows, s0_p, nseq=nb, seq=seq, row0=0, nb=min(nb, 4), tc=min(seq, 32))
    yp_s, s_s = _wkv_scan_one(rows, _state_to_tiles(state_wkv), nseq=nsample, row0=n_prompt)
    y_parts = [jnp.concatenate([a.reshape(n_prompt, -1), b], 0) for a, b in zip(yp_p, yp_s)]

    yo = _wkv_post(y_parts, r, k2, v, g, lnx_w[li], lnx_b[li], r_k[li].reshape(D))
    h = _mm(yo, w_o, layer=li, extras=[("full", h)], epilogue=lambda acc, hh: hh + acc, name="rwkv_o")
    return h, _tiles_to_state(s_p, H), _tiles_to_state(s_s, H), xn


def _ple_layer(h, p_all, w_proj, g_norm, w_gate, li):
    return _mm(h, w_gate, layer=li, norm_gain=g_norm, second=(p_all, w_proj, li, li), extras=[("full", h)],
               epilogue=lambda acc, pp, hh: hh + pp * jax.nn.sigmoid(acc), name="ple_gate")


def _shared_kv(h, pos, g_in, w_down, g_latent, dc, dr):
    M, D = h.shape
    cos_t, sin_t = _rope_tables(pos, dr)
    w_ext = jnp.concatenate([w_down, _swap_halves(w_down[:, dc:dc + dr], dr)], 1)

    assert 2 * dr == LANES

    def epilogue(acc, gl, ct, st):
        lat = _rms(acc[:, :dc], gl)
        t = acc[:, dc:]
        kpe = t * ct + pltpu.roll(t, dr, 1) * st
        out = jnp.concatenate([lat, kpe], 1)
        return out, out

    return _mm(h, w_ext, norm_gain=g_in,
               extras=[("const", g_latent.reshape(1, dc)), ("rows", cos_t), ("rows", sin_t)],
               epilogue=epilogue, out_dtypes=(F32, BF16), tn=dc + 2 * dr, name="kv_down")


def _mla_layer(h, pos, c_bf, n_prompt, nb, seq, cache_latent, cache_kpe_t, page_table, bw, li):
    b_norm, w_dq, g_q, w_uq, w_uk, w_uv, w_o = bw
    M, D = h.shape
    dc, nh, dn = w_uk.shape
    dv = w_uv.shape[2]
    ql = w_uq.shape[0]
    dr = w_uq.shape[1] // nh - dn
    scale = float(dn + dr) ** -0.5
    w_uq3 = w_uq.reshape(ql, nh, dn + dr)
    w_qn = w_uq3[:, :, :dn].reshape(ql, nh * dn)
    w_qr = w_uq3[:, :, dn:]
    assert 2 * dr == LANES
    w_qr_ext = jnp.concatenate([w_qr, _swap_halves(w_qr, dr)], -1).reshape(ql, nh * LANES)
    wuk_t = jnp.transpose(w_uk, (1, 2, 0)).astype(BF16)
    wuv_t = jnp.transpose(w_uv, (1, 0, 2)).astype(BF16)
    cos_t, sin_t = _rope_tables(pos, dr)

    cq = _mm(h, w_dq, layer=li, norm_gain=b_norm, extras=[("const", g_q.reshape(1, ql))], epilogue=_rms,
             out_dtypes=(BF16,), tn=ql, name="mla_dq")
    qn = _mm(cq, w_qn, out_dtypes=(BF16,), name="mla_uq_nope")

    def rope_epilogue(acc, ct, st):
        n = acc.shape[1]
        rot = acc * jnp.tile(ct, (1, nh)) + pltpu.roll(acc, n - dr, 1) * jnp.tile(st, (1, nh))
        keep = lax.broadcasted_iota(I32, acc.shape, 1) % LANES < dr
        return jnp.where(keep, rot * scale, 0.0)

    qp = _mm(cq, w_qr_ext, extras=[("rows", cos_t), ("rows", sin_t)], epilogue=rope_epilogue,
             out_dtypes=(BF16,), tn=nh * LANES, name="mla_uq_rope")

    o_p = _attn_prompt(qn, qp, c_bf, wuk_t, wuv_t, nb=nb, seq=seq, tq=min(seq, 128), tk=min(seq, 512),
                       scale=scale)
    ns = M - n_prompt
    qa_s = _head_mm(qn[n_prompt:], wuk_t, scale=scale).reshape(ns, nh, dc)
    ol_s = _attn_decode(qa_s, qp[n_prompt:].reshape(ns, nh, LANES), c_bf[n_prompt:].reshape(ns, 1, -1),
                        cache_latent, cache_kpe_t, page_table, dr=dr, npg=min(page_table.shape[1], 32))
    o_s = _head_mm(ol_s.reshape(ns, nh * dc), wuv_t)
    o = jnp.concatenate([o_p, o_s], 0)
    return _mm(o, w_o, layer=li, extras=[("full", h)], epilogue=lambda acc, hh: hh + acc, name="mla_o")


def kernel(x_prompt, x_sample, state_wkv, state_shift, cache_latent, cache_kpe, page_table, p_prompt, p_sample, a_norm, a_mu, a_wr, a_wk, a_wv, a_wo, a_w0, a_w1, a_w2, a_a0, a_a1, a_a2, a_g1, a_g2, a_kk, a_ka, a_rk, a_lnx_w, a_lnx_b, kv_norm, kv_wdown, kv_latent_norm, kv_wuk, kv_wuv, b_norm, b_wdq, b_qnorm, b_wuq, b_wo, f_norm, f_wgroup, f_bgroup, f_wexpert, f_bexpert, f_wgate, f_wup, f_wdown, pl_wproj, pl_norm, pl_wgate, final_norm):
    nb, seq, D = x_prompt.shape
    ns, dec_seq, _ = x_sample.shape
    assert dec_seq == 1
    depth = f_norm.shape[0]
    n_a = a_norm.shape[0]
    n_prompt = nb * seq
    dc = kv_latent_norm.shape[0]
    dr = kv_wdown.shape[1] - dc
    past_len = page_table.shape[1] * cache_latent.shape[1]
    pos = jnp.concatenate([jnp.tile(jnp.arange(seq, dtype=I32), nb), jnp.full((ns,), past_len, I32)])

    h = jnp.concatenate([x_prompt.reshape(n_prompt, D), x_sample.reshape(ns, D)], 0)
    p_all = jnp.concatenate([p_prompt.reshape(depth, n_prompt, -1), p_sample.reshape(depth, ns, -1)], 1)
    wkv_p, wkv_s, sh_p, sh_s = [], [], [], []
    c_f32 = c_bf = None
    cache_kpe_t = jnp.swapaxes(cache_kpe, 1, 2)
    aw = (a_norm, a_mu, a_wr, a_wk, a_wv, a_wo, a_w0, a_w1, a_w2, a_a0, a_a1, a_a2, a_g1, a_g2,
          a_kk, a_ka, a_rk, a_lnx_w, a_lnx_b)
    for i in range(depth):
        if i < n_a:
            h, s_p, s_s, xn = _rwkv_layer(h, state_wkv[i], state_shift[i], n_prompt, nb, seq, aw, i)
            wkv_p.append(s_p)
            wkv_s.append(s_s)
            sh_p.append(xn[seq - 1:n_prompt:seq])
            sh_s.append(xn[n_prompt:])
        else:
            j = i - n_a
            bw = (b_norm[j], b_wdq, b_qnorm[j], b_wuq[j], kv_wuk, kv_wuv, b_wo)
            h = _mla_layer(h, pos, c_bf, n_prompt, nb, seq, cache_latent, cache_kpe_t, page_table, bw, j)
        h = _moe(h, f_norm[i], f_wgroup[i], f_bgroup[i], f_wexpert[i], f_bexpert[i],
                 f_wgate, f_wup, f_wdown, i)
        h = _ple_layer(h, p_all, pl_wproj, pl_norm[i], pl_wgate, i)
        if i == n_a - 1:
            c_f32, c_bf = _shared_kv(h, pos, kv_norm, kv_wdown, kv_latent_norm, dc, dr)
    y_p = _norm(h, final_norm, F32, 0, n_prompt)
    y_s = _norm(h, final_norm, F32, n_prompt, ns)
    lat, kpe = c_f32[:, :dc], c_f32[:, dc:dc + dr]
    return (y_p.reshape(nb, seq, D), y_s.reshape(ns, 1, D),
            jnp.stack(wkv_p), jnp.stack(sh_p),
            lat[:n_prompt].reshape(nb, seq, dc), kpe[:n_prompt].reshape(nb, seq, dr),
            jnp.stack(wkv_s), jnp.stack(sh_s),
            lat[n_prompt:].reshape(ns, 1, dc), kpe[n_prompt:].reshape(ns, 1, dr))
```
